```python
import math
import jax, jax.numpy as jnp
from jax import lax
import numpy as np

D_MODEL = 1024
BATCH = 8
SEQ = 4096
DEPTH = 2

CTX_LEN = 256
GRID_W = 64
D_MIX = D_MODEL
D_MLSTM = D_MIX // 2
D_HYENA = D_MIX - D_MLSTM
MLSTM_HEADS = 4
MLSTM_HEAD_DIM = D_MLSTM // MLSTM_HEADS
MLSTM_CHUNK = 64
N_GATES = 4 * MLSTM_HEADS
SHORT_CONV = 3
HYENA_ORDER = 2
FILTER_BANDS = 8
FILTER_EMB = 1 + 2 * FILTER_BANDS
FILTER_HIDDEN = 64
DECAY_TARGET = 1e-2
FAST_DECAY_PCT = 0.3
SLOW_DECAY_PCT = 1.5
MIN_DECAY = math.log(DECAY_TARGET) / SLOW_DECAY_PCT
MAX_DECAY = math.log(DECAY_TARGET) / FAST_DECAY_PCT
RMS_EPS = 1e-6
SPLIT_SIZES = [2 * D_MLSTM, D_MLSTM, D_MLSTM, D_MLSTM, N_GATES, (HYENA_ORDER + 1) * D_HYENA, D_HYENA]
D_IN = sum(SPLIT_SIZES)

kernel_name = "hybrid_mlstm_hyena_prefix_dit"


def rms_norm(x, g):
    xf = x.astype(jnp.float32)
    y = xf * lax.rsqrt(jnp.mean(xf * xf, axis=-1, keepdims=True) + RMS_EPS)
    return (y * g.astype(jnp.float32)).astype(x.dtype)


def _conv3(u, w, b):
    p = jnp.pad(u, [(0, 0)] * (u.ndim - 2) + [(1, 1), (0, 0)])
    return p[..., :-2, :] * w[0] + p[..., 1:-1, :] * w[1] + p[..., 2:, :] * w[2] + b


def short_conv(u, w, b, rows):
    if rows is None:
        return _conv3(u, w, b)
    bsz, L, ch = u.shape
    return _conv3(u.reshape(bsz, rows, GRID_W, ch), w, b).reshape(bsz, L, ch)


def mlstm_chunkwise(q, k, v, i_pre, f_pre, state):
    bsz, nh, L, d = q.shape
    nc = L // MLSTM_CHUNK

    def to_chunks(a):
        a = a.reshape((bsz, nh, nc, MLSTM_CHUNK) + a.shape[3:])
        return jnp.moveaxis(a, 2, 0)

    logf = jax.nn.log_sigmoid(f_pre)
    lower = jnp.tril(jnp.ones((MLSTM_CHUNK, MLSTM_CHUNK), dtype=bool))

    def step(carry, inp):
        C, n, m = carry
        qc, kc, vc, ic, fc = inp
        bcum = jnp.cumsum(fc, axis=-1)
        dmat = bcum[..., :, None] - bcum[..., None, :] + ic[..., None, :]
        dmat = jnp.where(lower, dmat, -jnp.inf)
        inter = bcum + m[..., None]
        m_t = jnp.maximum(inter, jnp.max(dmat, axis=-1))
        s = jnp.einsum('bhtd,bhsd->bhts', qc, kc) * jnp.exp(dmat - m_t[..., None])
        w_inter = jnp.exp(inter - m_t)
        num = jnp.einsum('bhts,bhsv->bhtv', s, vc) + w_inter[..., None] * jnp.einsum('bhtk,bhkv->bhtv', qc, C)
        den = jnp.sum(s, axis=-1) + w_inter * jnp.einsum('bhtk,bhk->bht', qc, n)
        h = num / jnp.maximum(jnp.abs(den), jnp.exp(-m_t))[..., None]
        b_last = bcum[..., -1]
        src = b_last[..., None] - bcum + ic
        m_new = jnp.maximum(b_last + m, jnp.max(src, axis=-1))
        ws = jnp.exp(src - m_new[..., None])
        wc = jnp.exp(b_last + m - m_new)
        C_new = wc[..., None, None] * C + jnp.einsum('bhs,bhsk,bhsv->bhkv', ws, kc, vc)
        n_new = wc[..., None] * n + jnp.einsum('bhs,bhsk->bhk', ws, kc)
        return (C_new, n_new, m_new), h

    state, h = lax.scan(step, state, (to_chunks(q), to_chunks(k), to_chunks(v), to_chunks(i_pre), to_chunks(logf)))
    h = jnp.moveaxis(h, 0, 2).reshape(bsz, nh, L, d)
    return h, state


def mlstm_prep(u_qk, v, g, b_gates, conv_w, conv_b, rows):
    bsz, L, _ = v.shape
    qk = jax.nn.silu(short_conv(u_qk, conv_w, conv_b, rows).astype(jnp.float32))
    q, k = jnp.split(qk, 2, axis=-1)

    def heads(a):
        return a.reshape(bsz, L, MLSTM_HEADS, MLSTM_HEAD_DIM).transpose(0, 2, 1, 3).astype(jnp.float32)

    gates = (g + b_gates).astype(jnp.float32).reshape(bsz, L, 4, MLSTM_HEADS).transpose(2, 0, 3, 1)
    return heads(q), heads(k) * (MLSTM_HEAD_DIM ** -0.5), heads(v), gates


def mlstm_bidirectional(ctx_in, lat_in):
    qc, kc, vc, gc = ctx_in
    ql, kl, vl, gl = lat_in
    bsz = qc.shape[0]
    zero = (jnp.zeros((bsz, MLSTM_HEADS, MLSTM_HEAD_DIM, MLSTM_HEAD_DIM), jnp.float32),
            jnp.zeros((bsz, MLSTM_HEADS, MLSTM_HEAD_DIM), jnp.float32),
            jnp.zeros((bsz, MLSTM_HEADS), jnp.float32))
    flip = lambda a: jnp.flip(a, axis=2)
    hc_f, st_f = mlstm_chunkwise(qc, kc, vc, gc[0], gc[1], zero)
    hl_f, _ = mlstm_chunkwise(ql, kl, vl, gl[0], gl[1], st_f)
    hc_b, st_b = mlstm_chunkwise(flip(qc), flip(kc), flip(vc), flip(gc[2]), flip(gc[3]), zero)
    hl_b, _ = mlstm_chunkwise(flip(ql), flip(kl), flip(vl), flip(gl[2]), flip(gl[3]), st_b)
    return hc_f + flip(hc_b), hl_f + flip(hl_b)


def mlstm_output(h, o, zm, mh_gain):
    bsz, nh, L, d = h.shape
    h = h.transpose(0, 2, 1, 3) * jax.nn.sigmoid(o.astype(jnp.float32)).reshape(bsz, L, nh, d)
    h = h * lax.rsqrt(jnp.mean(h * h, axis=-1, keepdims=True) + RMS_EPS)
    h = h.reshape(bsz, L, nh * d) * mh_gain.astype(jnp.float32)
    return h * jax.nn.silu(zm.astype(jnp.float32))


def implicit_filters(L, w1, b1, freq, w2, b2, w3):
    f32 = lambda a: a.astype(jnp.float32)
    pos = jnp.arange(L, dtype=jnp.float32)
    bands = jnp.arange(1, FILTER_BANDS + 1, dtype=jnp.float32)
    ang = (2.0 * math.pi / L) * pos[:, None] * bands[None, :]
    feats = jnp.concatenate([(pos / L)[:, None], jnp.sin(ang), jnp.cos(ang)], axis=-1)
    hdn = jnp.sin(f32(freq[0]) * (feats @ f32(w1) + f32(b1)))
    hdn = jnp.sin(f32(freq[1]) * (hdn @ f32(w2) + f32(b2)))
    filt = (hdn @ f32(w3)).reshape(L, HYENA_ORDER, D_HYENA)
    offset = jnp.abs(pos - L // 2) / (L / 2)
    rates = jnp.abs(jnp.linspace(MIN_DECAY, MAX_DECAY, D_HYENA, dtype=jnp.float32))
    filt = filt * jnp.exp(-offset[:, None, None] * rates)
    return filt * lax.rsqrt(jnp.sum(filt * filt, axis=0, keepdims=True) + RMS_EPS)


def fft_long_conv(u, h, bias):
    L = u.shape[1]
    n = 2 * L
    uf = jnp.fft.rfft(u, n=n, axis=1)
    hf = jnp.fft.rfft(h, n=n, axis=0)
    y = jnp.fft.irfft(uf * hf[None], n=n, axis=1)[:, L // 2: L // 2 + L]
    return y + u * bias


def hyena_mixer(u_hy, zh, rows, conv_w, conv_b, filt, fbias):
    uv = short_conv(u_hy, conv_w, conv_b, rows).astype(jnp.float32)
    v, x1, x2 = jnp.split(uv, 3, axis=-1)
    fb = fbias.astype(jnp.float32)
    z = x1 * fft_long_conv(v, filt[:, 0], fb[0])
    y = x2 * fft_long_conv(z, filt[:, 1], fb[1])
    return y * jax.nn.silu(zh.astype(jnp.float32))


def split_proj(u):
    idx = [int(s) for s in np.cumsum(SPLIT_SIZES)[:-1]]
    return jnp.split(u, idx, axis=-1)


def setup_inputs(seed: int = 0) -> dict:
    key = jax.random.key(seed)
    ks = jax.random.split(key, 24)
    nrm = lambda k, shape, s: jax.random.normal(k, shape, jnp.float32) * s
    fbias = jnp.linspace(3.0, 6.0, MLSTM_HEADS, dtype=jnp.float32)
    zb = jnp.zeros_like(fbias)
    gate_offset = jnp.stack([zb, fbias, zb, fbias])[None]
    b_gates = (nrm(ks[9], (DEPTH, 4, MLSTM_HEADS), 0.1) + gate_offset).reshape(DEPTH, N_GATES)
    return {
        'x': nrm(ks[0], (BATCH, SEQ, D_MODEL), 1.0),
        'c': nrm(ks[1], (BATCH, D_MODEL), 1.0),
        'ctx': nrm(ks[2], (BATCH, CTX_LEN, D_MODEL), 1.0),
        'c_ctx': nrm(ks[3], (D_MODEL,), 1.0),
        'w_mod': nrm(ks[4], (DEPTH, D_MODEL, 3 * D_MODEL), 0.5 * D_MODEL ** -0.5),
        'b_mod': nrm(ks[5], (DEPTH, 3 * D_MODEL), 0.01),
        'pre_gain': 1.0 + nrm(ks[6], (DEPTH, D_MODEL), 0.05),
        'post_gain': 1.0 + nrm(ks[7], (DEPTH, D_MODEL), 0.05),
        'w_in': nrm(ks[8], (DEPTH, D_MODEL, D_IN), D_MODEL ** -0.5),
        'b_gates': b_gates,
        'conv_qk_w': nrm(ks[10], (DEPTH, SHORT_CONV, 2 * D_MLSTM), SHORT_CONV ** -0.5),
        'conv_qk_b': nrm(ks[11], (DEPTH, 2 * D_MLSTM), 0.01),
        'mh_gain': 1.0 + nrm(ks[12], (DEPTH, D_MLSTM), 0.05),
        'hy_conv_w': nrm(ks[13], (DEPTH, SHORT_CONV, (HYENA_ORDER + 1) * D_HYENA), SHORT_CONV ** -0.5),
        'hy_conv_b': nrm(ks[14], (DEPTH, (HYENA_ORDER + 1) * D_HYENA), 0.01),
        'filt_w1': nrm(ks[15], (DEPTH, FILTER_EMB, FILTER_HIDDEN), FILTER_EMB ** -0.5),
        'filt_b1': nrm(ks[16], (DEPTH, FILTER_HIDDEN), 0.01),
        'filt_freq': 1.0 + nrm(ks[17], (DEPTH, 2, FILTER_HIDDEN), 0.05),
        'filt_w2': nrm(ks[18], (DEPTH, FILTER_HIDDEN, FILTER_HIDDEN), FILTER_HIDDEN ** -0.5),
        'filt_b2': nrm(ks[19], (DEPTH, FILTER_HIDDEN), 0.01),
        'filt_w3': nrm(ks[20], (DEPTH, FILTER_HIDDEN, HYENA_ORDER * D_HYENA), FILTER_HIDDEN ** -0.5),
        'filt_bias': nrm(ks[21], (DEPTH, HYENA_ORDER, D_HYENA), 0.1),
        'w_out': nrm(ks[22], (DEPTH, D_MIX, D_MODEL), D_MIX ** -0.5),
    }


def reference(x, c, ctx, c_ctx, w_mod, b_mod, pre_gain, post_gain, w_in, b_gates,
              conv_qk_w, conv_qk_b, mh_gain, hy_conv_w, hy_conv_b, filt_w1, filt_b1,
              filt_freq, filt_w2, filt_b2, filt_w3, filt_bias, w_out):
    rows = x.shape[1] // GRID_W
    L_lat, L_ctx = x.shape[1], ctx.shape[1]
    xl, xc = x, ctx
    s_lat = jax.nn.silu(c)
    s_ctx = jax.nn.silu(c_ctx)
    for l in range(DEPTH):
        last = l == DEPTH - 1
        shift_l, scale_l, gate_l = jnp.split(s_lat @ w_mod[l] + b_mod[l], 3, axis=-1)
        shift_c, scale_c, gate_c = jnp.split(s_ctx @ w_mod[l] + b_mod[l], 3, axis=-1)
        hl = rms_norm(xl, pre_gain[l]) * (1.0 + scale_l[:, None]) + shift_l[:, None]
        hc = rms_norm(xc, pre_gain[l]) * (1.0 + scale_c) + shift_c
        qk_l, v_l, o_l, zm_l, g_l, hy_l, zh_l = split_proj(hl @ w_in[l])
        qk_c, v_c, o_c, zm_c, g_c, hy_c, zh_c = split_proj(hc @ w_in[l])
        ctx_in = mlstm_prep(qk_c, v_c, g_c, b_gates[l], conv_qk_w[l], conv_qk_b[l], None)
        lat_in = mlstm_prep(qk_l, v_l, g_l, b_gates[l], conv_qk_w[l], conv_qk_b[l], rows)
        hm_c, hm_l = mlstm_bidirectional(ctx_in, lat_in)
        m_l = mlstm_output(hm_l, o_l, zm_l, mh_gain[l])
        filt_l = implicit_filters(L_lat, filt_w1[l], filt_b1[l], filt_freq[l], filt_w2[l], filt_b2[l], filt_w3[l])
        y_hy_l = hyena_mixer(hy_l, zh_l, rows, hy_conv_w[l], hy_conv_b[l], filt_l, filt_bias[l])
        y_l = jnp.concatenate([m_l, y_hy_l], axis=-1).astype(xl.dtype) @ w_out[l]
        new_xl = xl + gate_l[:, None] * rms_norm(y_l, post_gain[l])
        if not last:
            m_c = mlstm_output(hm_c, o_c, zm_c, mh_gain[l])
            filt_c = implicit_filters(L_ctx, filt_w1[l], filt_b1[l], filt_freq[l], filt_w2[l], filt_b2[l], filt_w3[l])
            y_hy_c = hyena_mixer(hy_c, zh_c, None, hy_conv_w[l], hy_conv_b[l], filt_c, filt_bias[l])
            y_c = jnp.concatenate([m_c, y_hy_c], axis=-1).astype(xc.dtype) @ w_out[l]
            xc = xc + gate_c * rms_norm(y_c, post_gain[l])
        xl = new_xl
    return xl
```

```python
import functools
import math

import jax
import jax.numpy as jnp
from jax import lax
from jax.experimental import pallas as pl
from jax.experimental.pallas import tpu as pltpu

F32 = jnp.float32
BF16 = jnp.bfloat16
HI = lax.Precision.HIGHEST

LANES = 128
SUBLANES = 8
MXU_DIM = 256
VMEM_LIMIT = 56 * 1024 * 1024

GRID_W = 64
MLSTM_HEADS = 4
HYENA_ORDER = 2
FILTER_BANDS = 8
DECAY_TARGET = 1e-2
FAST_DECAY_PCT = 0.3
SLOW_DECAY_PCT = 1.5
MIN_DECAY = math.log(DECAY_TARGET) / SLOW_DECAY_PCT
MAX_DECAY = math.log(DECAY_TARGET) / FAST_DECAY_PCT
RMS_EPS = 1e-6

TBLK = MXU_DIM
CONV_CB = 2 * SUBLANES
GATE_PAD = LANES
NT_DIMS = (((1,), (1,)), ((), ()))


def _params(*sem):
    return pltpu.CompilerParams(dimension_semantics=sem, vmem_limit_bytes=VMEM_LIMIT)


def _silu(x):
    return x * jax.nn.sigmoid(x)


def _mod_kernel(c_ref, w_ref, b_ref, o_ref):
    s = _silu(c_ref[...])
    o_ref[0] = jnp.dot(s, w_ref[0], precision=HI, preferred_element_type=F32) + b_ref[0]


def _modulation(cc, w_mod, b_mod):
    depth, d, d3 = w_mod.shape
    rows = cc.shape[0]
    nb = d3 // d
    return pl.pallas_call(
        _mod_kernel,
        grid=(depth, nb),
        in_specs=[
            pl.BlockSpec((rows, d), lambda l, n: (0, 0)),
            pl.BlockSpec((1, d, d), lambda l, n: (l, 0, n)),
            pl.BlockSpec((1, 1, d), lambda l, n: (l, 0, n)),
        ],
        out_specs=pl.BlockSpec((1, rows, d), lambda l, n: (l, 0, n)),
        out_shape=jax.ShapeDtypeStruct((depth, rows, d3), F32),
        compiler_params=_params("arbitrary", "arbitrary"),
        name="modulation",
    )(cc, w_mod, b_mod.reshape(depth, 1, d3))


def _inproj_kernel(x_ref, sc_ref, sh_ref, pg_ref, wn_ref, wt_ref, cqw_ref, cqb_ref, chw_ref, chb_ref, bg_ref,
                   q_ref, k_ref, v_ref, o_ref, zm_ref, g_ref, hy_ref, *, group, dm, dh):
    x = x_ref[0]
    tm = x.shape[0]
    ms = jnp.mean(x * x, axis=-1, keepdims=True)
    y = x * lax.rsqrt(ms + RMS_EPS) * pg_ref[...]
    h = (y * (1.0 + sc_ref[0]) + sh_ref[0]).astype(BF16)
    un = jnp.dot(h, wn_ref[...], preferred_element_type=F32)
    ut = lax.dot_general(wt_ref[...], h, NT_DIMS, preferred_element_type=F32)

    uqk = un[:, :2 * dm]
    tg = lax.broadcasted_iota(jnp.int32, (tm, 1), 0) % group
    prev = jnp.where(tg == 0, 0.0, pltpu.roll(uqk, 1, axis=0))
    nxt = jnp.where(tg == group - 1, 0.0, pltpu.roll(uqk, tm - 1, axis=0))
    cq = prev * cqw_ref[0:1, :] + uqk * cqw_ref[1:2, :] + nxt * cqw_ref[2:3, :] + cqb_ref[...]
    sq = _silu(cq)
    hd = dm // MLSTM_HEADS
    q_ref[0] = sq[:, :dm].astype(BF16)
    k_ref[0] = (sq[:, dm:] * (hd ** -0.5)).astype(BF16)
    v_ref[0] = un[:, 2 * dm:3 * dm].astype(BF16)
    o_ref[0] = un[:, 3 * dm:4 * dm]
    zm_ref[0] = un[:, 4 * dm:5 * dm]
    g_ref[0] = un[:, 5 * dm:] + bg_ref[...]

    uh = ut[:3 * dh]
    lg = lax.broadcasted_iota(jnp.int32, (1, tm), 1) % group
    prevh = jnp.where(lg == 0, 0.0, pltpu.roll(uh, 1, axis=1))
    nxth = jnp.where(lg == group - 1, 0.0, pltpu.roll(uh, tm - 1, axis=1))
    ch = prevh * chw_ref[:, 0:1] + uh * chw_ref[:, 1:2] + nxth * chw_ref[:, 2:3] + chb_ref[...]
    full = jnp.concatenate([ch, ut[3 * dh:]], axis=0)
    for half in range(tm // LANES):
        hy_ref[:, half, 0, 0] = full[:, half * LANES:(half + 1) * LANES].reshape(4 * dh // CONV_CB, CONV_CB, LANES)


def _inproj(x, scale, shift, pre_gain, wn, wt, cqw, cqb, chw, chb, bg, *, group):
    bsz, seq, d = x.shape
    tm = TBLK
    nblk = seq // tm
    dm = (wn.shape[1] - GATE_PAD) // 5
    dh = wt.shape[0] // 4
    row = lambda b, t: (b, t, 0)
    const2 = lambda b, t: (0, 0)
    kern = functools.partial(_inproj_kernel, group=group, dm=dm, dh=dh)
    return pl.pallas_call(
        kern,
        grid=(bsz, nblk),
        in_specs=[
            pl.BlockSpec((1, tm, d), row),
            pl.BlockSpec((1, 1, d), lambda b, t: (b, 0, 0)),
            pl.BlockSpec((1, 1, d), lambda b, t: (b, 0, 0)),
            pl.BlockSpec((1, d), const2),
            pl.BlockSpec(wn.shape, const2),
            pl.BlockSpec(wt.shape, const2),
            pl.BlockSpec(cqw.shape, const2),
            pl.BlockSpec(cqb.shape, const2),
            pl.BlockSpec(chw.shape, const2),
            pl.BlockSpec(chb.shape, const2),
            pl.BlockSpec(bg.shape, const2),
        ],
        out_specs=[
            pl.BlockSpec((1, tm, dm), row),
            pl.BlockSpec((1, tm, dm), row),
            pl.BlockSpec((1, tm, dm), row),
            pl.BlockSpec((1, tm, dm), row),
            pl.BlockSpec((1, tm, dm), row),
            pl.BlockSpec((1, tm, GATE_PAD), row),
            pl.BlockSpec((4 * dh // CONV_CB, tm // LANES, 1, 1, CONV_CB, LANES), lambda b, t: (0, 0, t, b, 0, 0)),
        ],
        out_shape=[
            jax.ShapeDtypeStruct((bsz, seq, dm), BF16),
            jax.ShapeDtypeStruct((bsz, seq, dm), BF16),
            jax.ShapeDtypeStruct((bsz, seq, dm), BF16),
            jax.ShapeDtypeStruct((bsz, seq, dm), F32),
            jax.ShapeDtypeStruct((bsz, seq, dm), F32),
            jax.ShapeDtypeStruct((bsz, seq, GATE_PAD), F32),
            jax.ShapeDtypeStruct((4 * dh // CONV_CB, tm // LANES, nblk, bsz, CONV_CB, LANES), F32),
        ],
        compiler_params=_params("arbitrary", "arbitrary"),
        name="inproj",
    )(x, scale, shift, pre_gain, wn, wt, cqw, cqb, chw, chb, bg)


def _mlstm_kernel(q_ref, k_ref, v_ref, g_ref, o_ref, zm_ref, gain_ref, c0_ref, n0_ref, m0_ref,
                  out_ref, cf_ref, nf_ref, mf_ref, hb_ref, c_ref, n_ref, m_ref, *, lc, nc, nh, dh):
    phase = pl.program_id(1)
    j = pl.program_id(2)

    def run(rev):
        d = 1 if rev else 0

        @pl.when(j == 0)
        def _():
            c_ref[...] = c0_ref[0, d]
            n_ref[...] = n0_ref[0, d]
            m_ref[...] = m0_ref[0, d]

        row0 = pl.multiple_of(((nc - 1 - j) if rev else j) * lc, lc)
        g = g_ref[0]
        lf = jax.nn.log_sigmoid(g)
        r = lax.broadcasted_iota(jnp.int32, (lc, lc), 0)
        s = lax.broadcasted_iota(jnp.int32, (lc, lc), 1)
        causal = (s >= r) if rev else (s <= r)
        bcum = jnp.dot(causal.astype(F32), lf, precision=HI, preferred_element_type=F32)
        tot = jnp.sum(lf, axis=0, keepdims=True)
        gt = g.T
        bt = bcum.T
        for h in range(nh):
            ic = 2 * nh * d + h
            fc = ic + nh
            hs = slice(h * dh, (h + 1) * dh)
            bcol = bcum[:, fc:fc + 1]
            brow = bt[fc:fc + 1, :]
            irow = gt[ic:ic + 1, :]
            icol = g[:, ic:ic + 1]
            toth = tot[:, fc:fc + 1]
            mprev = m_ref[h][:, 0:1]
            dmat = jnp.where(causal, bcol - brow + irow, -jnp.inf)
            inter = bcol + mprev
            mt = jnp.maximum(inter, jnp.max(dmat, axis=1, keepdims=True))
            qh = q_ref[0, :, hs]
            kh = k_ref[0, :, hs]
            vh = v_ref[0, :, hs]
            sm = lax.dot_general(qh, kh, NT_DIMS, preferred_element_type=F32) * jnp.exp(dmat - mt)
            wint = jnp.exp(inter - mt)
            cst = c_ref[h]
            nst = n_ref[h]
            num = (jnp.dot(sm.astype(BF16), vh, preferred_element_type=F32)
                   + wint * jnp.dot(qh, cst.astype(BF16), preferred_element_type=F32))
            den = (jnp.sum(sm, axis=1, keepdims=True)
                   + wint * jnp.sum(qh.astype(F32) * nst, axis=1, keepdims=True))
            hh = num / jnp.maximum(jnp.abs(den), jnp.exp(-mt))

            src = toth - bcol + icol
            mnew = jnp.maximum(toth + mprev, jnp.max(src, axis=0, keepdims=True))
            ws = jnp.exp(src - mnew)
            wc = jnp.exp(toth + mprev - mnew)
            kw = kh.astype(F32) * ws
            c_ref[h] = wc * cst + jnp.dot(kw.T.astype(BF16), vh, preferred_element_type=F32)
            n_ref[h] = wc * nst + jnp.sum(kw, axis=0, keepdims=True)
            m_ref[h] = jnp.broadcast_to(mnew, (1, dh))

            if rev:
                hb_ref[pl.ds(row0, lc), hs] = hh
            else:
                ho = (hh + hb_ref[pl.ds(row0, lc), hs]) * jax.nn.sigmoid(o_ref[0, :, hs])
                ho = ho * lax.rsqrt(jnp.mean(ho * ho, axis=-1, keepdims=True) + RMS_EPS)
                out_ref[0, :, hs] = ho * gain_ref[:, hs] * _silu(zm_ref[0, :, hs])

        @pl.when(j == nc - 1)
        def _():
            cf_ref[0, d] = c_ref[...]
            nf_ref[0, d] = n_ref[...]
            mf_ref[0, d] = m_ref[...]

    pl.when(phase == 0)(functools.partial(run, True))
    pl.when(phase == 1)(functools.partial(run, False))


def _mlstm(q, k, v, g, o, zm, gain, c0, n0, m0):
    bsz, seq, dmw = q.shape
    nh = MLSTM_HEADS
    dh = dmw // nh
    lc = min(seq, MXU_DIM)
    nc = seq // lc
    chunk = lambda b, p, j: (b, jnp.where(p == 0, nc - 1 - j, j), 0)
    fwd_only = lambda b, p, j: (b, j * p, 0)
    st5 = lambda b, p, j: (b, 0, 0, 0, 0)
    kern = functools.partial(_mlstm_kernel, lc=lc, nc=nc, nh=nh, dh=dh)
    return pl.pallas_call(
        kern,
        grid=(bsz, 2, nc),
        in_specs=[
            pl.BlockSpec((1, lc, dmw), chunk),
            pl.BlockSpec((1, lc, dmw), chunk),
            pl.BlockSpec((1, lc, dmw), chunk),
            pl.BlockSpec((1, lc, GATE_PAD), chunk),
            pl.BlockSpec((1, lc, dmw), fwd_only),
            pl.BlockSpec((1, lc, dmw), fwd_only),
            pl.BlockSpec((1, dmw), lambda b, p, j: (0, 0)),
            pl.BlockSpec((1, 2, nh, dh, dh), st5),
            pl.BlockSpec((1, 2, nh, 1, dh), st5),
            pl.BlockSpec((1, 2, nh, 1, dh), st5),
        ],
        out_specs=[
            pl.BlockSpec((1, lc, dmw), fwd_only),
            pl.BlockSpec((1, 2, nh, dh, dh), st5),
            pl.BlockSpec((1, 2, nh, 1, dh), st5),
            pl.BlockSpec((1, 2, nh, 1, dh), st5),
        ],
        out_shape=[
            jax.ShapeDtypeStruct((bsz, seq, dmw), F32),
            jax.ShapeDtypeStruct((bsz, 2, nh, dh, dh), F32),
            jax.ShapeDtypeStruct((bsz, 2, nh, 1, dh), F32),
            jax.ShapeDtypeStruct((bsz, 2, nh, 1, dh), F32),
        ],
        scratch_shapes=[
            pltpu.VMEM((seq, dmw), F32),
            pltpu.VMEM((nh, dh, dh), F32),
            pltpu.VMEM((nh, 1, dh), F32),
            pltpu.VMEM((nh, 1, dh), F32),
        ],
        compiler_params=_params("arbitrary", "arbitrary", "arbitrary"),
        name="mlstm",
    )(q, k, v, g, o, zm, gain, c0, n0, m0)


def _conv_geometry(seq):
    nblk = seq // TBLK
    dmax = min(nblk - 1, (seq // 2 + TBLK - 2) // TBLK)
    ncols = (2 * dmax + 1) * TBLK
    n_base = seq // 2 + TBLK * (1 - dmax)
    lpad = n_base + ncols
    return nblk, dmax, ncols, n_base, lpad


def _filter_kernel(w1_ref, b1_ref, fr_ref, w2_ref, b2_ref, w3_ref, rate_ref, o_ref, h2_ref, *, seq, lpad):
    pos = lax.broadcasted_iota(jnp.int32, (1, seq), 1).astype(F32)

    @pl.when(pl.program_id(0) == 0)
    def _():
        bands = (lax.broadcasted_iota(jnp.int32, (FILTER_BANDS, 1), 0) + 1).astype(F32)
        ang = (2.0 * math.pi / seq) * pos * bands
        sn = jnp.sin(ang)
        cs = jnp.cos(ang)
        pre = b1_ref[...] + w1_ref[:, 0:1] * (pos / seq)
        for kb in range(FILTER_BANDS):
            pre = pre + w1_ref[:, 1 + kb:2 + kb] * sn[kb:kb + 1, :]
            pre = pre + w1_ref[:, 1 + FILTER_BANDS + kb:2 + FILTER_BANDS + kb] * cs[kb:kb + 1, :]
        h1 = jnp.sin(fr_ref[:, 0:1] * pre)
        h2_ref[...] = jnp.sin(fr_ref[:, 1:2] * (
            jnp.dot(w2_ref[...], h1, precision=HI, preferred_element_type=F32) + b2_ref[...]))

    f = jnp.dot(w3_ref[...], h2_ref[...], precision=HI, preferred_element_type=F32)
    offset = jnp.abs(pos - (seq // 2)) / (seq / 2)
    f = f * jnp.exp(-offset * rate_ref[...])
    f = f * lax.rsqrt(jnp.sum(f * f, axis=1, keepdims=True) + RMS_EPS)
    rows = f.shape[0]
    o_ref[:, :TBLK] = jnp.zeros((rows, TBLK), F32)
    o_ref[:, TBLK:TBLK + seq] = f
    o_ref[:, TBLK + seq:] = jnp.zeros((rows, lpad - TBLK - seq), F32)


def _filters(seq, w1, b1, freq, w2, b2, w3):
    hidden = w2.shape[0]
    nrow = w3.shape[1]
    c = nrow // HYENA_ORDER
    _, _, _, _, lpad = _conv_geometry(seq)
    rates = jnp.abs(jnp.linspace(MIN_DECAY, MAX_DECAY, c, dtype=F32))
    rates = jnp.tile(rates, HYENA_ORDER).reshape(nrow, 1)
    rb = LANES
    const = lambda i: (0, 0)
    out = pl.pallas_call(
        functools.partial(_filter_kernel, seq=seq, lpad=lpad),
        grid=(nrow // rb,),
        in_specs=[
            pl.BlockSpec((hidden, w1.shape[0]), const),
            pl.BlockSpec((hidden, 1), const),
            pl.BlockSpec((hidden, 2), const),
            pl.BlockSpec((hidden, hidden), const),
            pl.BlockSpec((hidden, 1), const),
            pl.BlockSpec((rb, hidden), lambda i: (i, 0)),
            pl.BlockSpec((rb, 1), lambda i: (i, 0)),
        ],
        out_specs=pl.BlockSpec((rb, lpad), lambda i: (i, 0)),
        out_shape=jax.ShapeDtypeStruct((nrow, lpad), F32),
        scratch_shapes=[pltpu.VMEM((hidden, seq), F32)],
        compiler_params=_params("arbitrary"),
        name="hyena_filter",
    )(w1.T, b1.reshape(hidden, 1), freq.T, w2.T, b2.reshape(hidden, 1), w3.T, rates)
    return out.reshape(nrow, lpad // LANES, LANES)


def _longconv_kernel(*refs, m, mp, bsz, cb, dmax, nj, jt0, nb, with_zh):
    if with_zh:
        fb_ref, u_ref, x_ref, zh_ref, f_ref, o_ref, p_ref, a_ref = refs
    else:
        fb_ref, u_ref, x_ref, f_ref, o_ref, p_ref, a_ref = refs
        zh_ref = None
    mc = m * cb
    pad = bsz * dmax
    base = pl.program_id(0) * cb
    p_ref[...] = jnp.zeros(p_ref.shape, F32)
    lane = lax.broadcasted_iota(jnp.int32, (LANES, LANES), 1)
    sub = lax.broadcasted_iota(jnp.int32, (LANES, LANES), 0)
    keep = lane >= sub

    def rows(ref, c):
        return jnp.concatenate([ref[0, pl.ds(hf * mc + c, m, stride=cb), :] for hf in range(TBLK // LANES)], axis=1)

    def channel(c, carry):
        u = rows(u_ref, c)
        p_ref[pad:pad + m, :] = u
        prev = None
        for jj in range(jt0 - 2, jt0 + nb):
            tap = jnp.broadcast_to(f_ref[0, pl.ds(c * nj + jj, 1), :], (LANES, LANES))
            cur = pltpu.roll(tap, 0, axis=1, stride=1, stride_axis=0)
            if prev is not None:
                blk = jnp.where(keep, cur, prev).astype(BF16)
                if jj >= jt0:
                    a_ref[:LANES, (jj - jt0) * LANES:(jj - jt0 + 1) * LANES] = blk
                if jj + 1 < jt0 + nb:
                    a_ref[LANES:, (jj + 1 - jt0) * LANES:(jj + 2 - jt0) * LANES] = blk
            prev = cur
        acc = jnp.zeros((mp, TBLK), F32)
        for d in range(-dmax, dmax + 1):
            lhs = p_ref[pad - bsz * d:pad - bsz * d + mp, :].astype(BF16)
            col = (d + dmax) * TBLK
            acc = acc + jnp.dot(lhs, a_ref[:, col:col + TBLK], preferred_element_type=F32)
        y = rows(x_ref, c) * (acc[:m] + u * fb_ref[base + c])
        if with_zh:
            y = y * _silu(rows(zh_ref, c))
        for hf in range(TBLK // LANES):
            o_ref[0, pl.ds(hf * mc + c, m, stride=cb), :] = y[:, hf * LANES:(hf + 1) * LANES]
        return carry

    lax.fori_loop(0, cb, channel, 0)


def _longconv(u, u_off, x, x_off, zh, zh_off, filt, f_off, fbias, seq):
    _, nhalf, nblk, bsz, cb, _ = u.shape
    c = fbias.shape[0]
    _, dmax, ncols, n_base, lpad = _conv_geometry(seq)
    nj = lpad // LANES
    m = nblk * bsz
    mp = max(m, 2 * SUBLANES)
    pad = bsz * dmax
    with_zh = zh is not None
    rows = nhalf * m * cb
    flat = lambda a: a.reshape(a.shape[0], rows, LANES)
    spec = lambda off: pl.BlockSpec((1, rows, LANES), lambda i: (off // cb + i, 0, 0))
    in_specs = [pl.BlockSpec(memory_space=pltpu.SMEM), spec(u_off), spec(x_off)]
    args = [fbias, flat(u), flat(x)]
    if with_zh:
        in_specs.append(spec(zh_off))
        args.append(flat(zh))
    in_specs.append(pl.BlockSpec((1, cb * nj, LANES), lambda i: (f_off // cb + i, 0, 0)))
    args.append(filt.reshape(filt.shape[0] // cb, cb * nj, LANES))
    kern = functools.partial(_longconv_kernel, m=m, mp=mp, bsz=bsz, cb=cb, dmax=dmax, nj=nj,
                             jt0=n_base // LANES, nb=ncols // LANES, with_zh=with_zh)
    out = pl.pallas_call(
        kern,
        grid=(c // cb,),
        in_specs=in_specs,
        out_specs=pl.BlockSpec((1, rows, LANES), lambda i: (i, 0, 0)),
        out_shape=jax.ShapeDtypeStruct((c // cb, rows, LANES), F32),
        scratch_shapes=[
            pltpu.VMEM((2 * pad + mp, TBLK), F32),
            pltpu.VMEM((TBLK, ncols), BF16),
        ],
        compiler_params=_params("arbitrary"),
        name="hyena_longconv",
    )(*args)
    return out.reshape(c // cb, nhalf, nblk, bsz, cb, LANES)


def _outproj_kernel(m_ref, yh_ref, x_ref, wa_ref, wb_ref, pg_ref, gate_ref, o_ref):
    mm = m_ref[0].astype(BF16)
    ncb, nhalf, _, _, cb, _ = yh_ref.shape
    yh = jnp.concatenate([yh_ref[:, hf, 0, 0].reshape(ncb * cb, LANES) for hf in range(nhalf)], axis=1)
    y = (jnp.dot(mm, wa_ref[...], preferred_element_type=F32)
         + jnp.dot(yh.T.astype(BF16), wb_ref[...], preferred_element_type=F32))
    r = y * lax.rsqrt(jnp.mean(y * y, axis=-1, keepdims=True) + RMS_EPS) * pg_ref[...]
    o_ref[0] = x_ref[0] + gate_ref[0] * r


def _outproj(mm, yh, x, wa, wb, post_gain, gate):
    bsz, seq, d = x.shape
    tm = TBLK
    dm = mm.shape[2]
    ncb, nhalf, _, _, cb, _ = yh.shape
    row = lambda b, t: (b, t, 0)
    const2 = lambda b, t: (0, 0)
    return pl.pallas_call(
        _outproj_kernel,
        grid=(bsz, seq // tm),
        in_specs=[
            pl.BlockSpec((1, tm, dm), row),
            pl.BlockSpec((ncb, nhalf, 1, 1, cb, LANES), lambda b, t: (0, 0, t, b, 0, 0)),
            pl.BlockSpec((1, tm, d), row),
            pl.BlockSpec(wa.shape, const2),
            pl.BlockSpec(wb.shape, const2),
            pl.BlockSpec((1, d), const2),
            pl.BlockSpec((1, 1, d), lambda b, t: (b, 0, 0)),
        ],
        out_specs=pl.BlockSpec((1, tm, d), row),
        out_shape=jax.ShapeDtypeStruct((bsz, seq, d), F32),
        compiler_params=_params("arbitrary", "arbitrary"),
        name="outproj",
    )(mm, yh, x, wa, wb, post_gain, gate)


def _mixer(x, scale, shift, gate, state, lw, *, group, need_out):
    seq = x.shape[1]
    q, k, v, o, zm, g, hy = _inproj(x, scale, shift, lw["pre_gain"], lw["wn"], lw["wt"], lw["cqw"], lw["cqb"],
                                    lw["chw"], lw["chb"], lw["bg"], group=group)
    mm, cf, nf, mf = _mlstm(q, k, v, g, o, zm, lw["mh_gain"], *state)
    if not need_out:
        return None, (cf, nf, mf)
    dh = lw["fbias"].shape[1]
    filt = _filters(seq, lw["fw1"], lw["fb1"], lw["ffreq"], lw["fw2"], lw["fb2"], lw["fw3"])
    z = _longconv(hy, 0, hy, dh, None, 0, filt, 0, lw["fbias"][0], seq)
    yh = _longconv(z, 0, hy, 2 * dh, hy, 3 * dh, filt, dh, lw["fbias"][1], seq)
    new_x = _outproj(mm, yh, x, lw["wa"], lw["wb"], lw["post_gain"], gate)
    return new_x, (cf, nf, mf)


def kernel(x, c, ctx, c_ctx, w_mod, b_mod, pre_gain, post_gain, w_in, b_gates, conv_qk_w, conv_qk_b, mh_gain,
           hy_conv_w, hy_conv_b, filt_w1, filt_b1, filt_freq, filt_w2, filt_b2, filt_w3, filt_bias, w_out):
    bsz, _, d = x.shape
    depth = w_mod.shape[0]
    dm = mh_gain.shape[1]
    dh = filt_bias.shape[2]
    nh = MLSTM_HEADS
    hd = dm // nh
    ngate = b_gates.shape[1]
    o_gate = 5 * dm
    o_hy = o_gate + ngate

    mod_rows = 2 * SUBLANES
    cc = jnp.concatenate([c, c_ctx[None, :], jnp.zeros((mod_rows - bsz - 1, d), F32)], axis=0)
    mod = _modulation(cc, w_mod, b_mod)

    zero_state = (jnp.zeros((bsz, 2, nh, hd, hd), F32), jnp.zeros((bsz, 2, nh, 1, hd), F32),
                  jnp.zeros((bsz, 2, nh, 1, hd), F32))
    xl, xc = x, ctx
    for l in range(depth):
        last = l == depth - 1
        w_l = w_in[l].astype(BF16)
        gate_w = jnp.pad(w_l[:, o_gate:o_hy], ((0, 0), (0, GATE_PAD - ngate)))
        lw = dict(
            pre_gain=pre_gain[l].reshape(1, d), post_gain=post_gain[l].reshape(1, d),
            wn=jnp.concatenate([w_l[:, :o_gate], gate_w], axis=1), wt=w_l[:, o_hy:].T,
            cqw=conv_qk_w[l], cqb=conv_qk_b[l].reshape(1, 2 * dm),
            chw=hy_conv_w[l].T, chb=hy_conv_b[l].reshape(3 * dh, 1),
            bg=jnp.pad(b_gates[l], (0, GATE_PAD - ngate)).reshape(1, GATE_PAD),
            mh_gain=mh_gain[l].reshape(1, dm),
            fw1=filt_w1[l], fb1=filt_b1[l], ffreq=filt_freq[l], fw2=filt_w2[l], fb2=filt_b2[l], fw3=filt_w3[l],
            fbias=filt_bias[l], wa=w_out[l, :dm].astype(BF16), wb=w_out[l, dm:].astype(BF16),
        )
        ml = mod[l]
        split3 = lambda a: (a[..., :d], a[..., d:2 * d], a[..., 2 * d:])
        shift_l, scale_l, gate_l = split3(ml[:bsz].reshape(bsz, 1, 3 * d))
        shift_c, scale_c, gate_c = split3(jnp.broadcast_to(ml[bsz].reshape(1, 1, 3 * d), (bsz, 1, 3 * d)))
        new_xc, ctx_state = _mixer(xc, scale_c, shift_c, gate_c, zero_state, lw, group=xc.shape[1], need_out=not last)
        xl, _ = _mixer(xl, scale_l, shift_l, gate_l, ctx_state, lw, group=GRID_W, need_out=True)
        if not last:
            xc = new_xc
    return xl
```

```python
import functools
import math

import jax
import jax.numpy as jnp
from jax import lax
from jax.experimental import pallas as pl
from jax.experimental.pallas import tpu as pltpu

F32 = jnp.float32
BF16 = jnp.bfloat16
HI = lax.Precision.HIGHEST

LANES = 128
SUBLANES = 8
BF16_ROWS = 2 * SUBLANES
MXU_DIM = 256
VMEM_LIMIT = 56 * 1024 * 1024

GRID_W = 64
MLSTM_HEADS = 4
HYENA_ORDER = 2
FILTER_BANDS = 8
DECAY_TARGET = 1e-2
FAST_DECAY_PCT = 0.3
SLOW_DECAY_PCT = 1.5
MIN_DECAY = math.log(DECAY_TARGET) / SLOW_DECAY_PCT
MAX_DECAY = math.log(DECAY_TARGET) / FAST_DECAY_PCT
RMS_EPS = 1e-6

TBLK = MXU_DIM
CONV_CB = 2 * SUBLANES
INPROJ_ROWS = TBLK
GATE_PAD = LANES
MLSTM_CHUNK = LANES
MLSTM_BLOCK = 4 * MLSTM_CHUNK
LOG2E = math.log2(math.e)
NT_DIMS = (((1,), (1,)), ((), ()))


def _params(*sem):
    return pltpu.CompilerParams(dimension_semantics=sem, vmem_limit_bytes=VMEM_LIMIT)


def _silu(x):
    return x * jax.nn.sigmoid(x)


def _mod_kernel(c_ref, w_ref, b_ref, o_ref):
    s = _silu(c_ref[...])
    o_ref[0] = jnp.dot(s, w_ref[0], precision=HI, preferred_element_type=F32) + b_ref[0]


def _modulation(cc, w_mod, b_mod):
    depth, d, d3 = w_mod.shape
    rows = cc.shape[0]
    nb = d3 // d
    return pl.pallas_call(
        _mod_kernel,
        grid=(depth, nb),
        in_specs=[
            pl.BlockSpec((rows, d), lambda l, n: (0, 0)),
            pl.BlockSpec((1, d, d), lambda l, n: (l, 0, n)),
            pl.BlockSpec((1, 1, d), lambda l, n: (l, 0, n)),
        ],
        out_specs=pl.BlockSpec((1, rows, d), lambda l, n: (l, 0, n)),
        out_shape=jax.ShapeDtypeStruct((depth, rows, d3), F32),
        compiler_params=_params("arbitrary", "arbitrary"),
        name="modulation",
    )(cc, w_mod, b_mod.reshape(depth, 1, d3))


def _inproj_kernel(x_ref, sc_ref, sh_ref, pg_ref, wn_ref, wt_ref, cqw_ref, cqb_ref, chw_ref, chb_ref, bg_ref,
                   mg_ref, q_ref, kt_ref, v_ref, og_ref, zg_ref, g_ref, hy_ref, *, group, dm, dh):
    x = x_ref[0]
    tm = x.shape[0]
    ms = jnp.mean(x * x, axis=-1, keepdims=True)
    y = x * lax.rsqrt(ms + RMS_EPS) * pg_ref[...]
    h = (y * (1.0 + sc_ref[0]) + sh_ref[0]).astype(BF16)
    un = jnp.dot(h, wn_ref[...], preferred_element_type=F32)
    ut = lax.dot_general(wt_ref[...], h, NT_DIMS, preferred_element_type=F32)

    uqk = un[:, :2 * dm]
    tg = lax.broadcasted_iota(jnp.int32, (tm, 1), 0) % group
    prev = jnp.where(tg == 0, 0.0, pltpu.roll(uqk, 1, axis=0))
    nxt = jnp.where(tg == group - 1, 0.0, pltpu.roll(uqk, tm - 1, axis=0))
    cq = prev * cqw_ref[0:1, :] + uqk * cqw_ref[1:2, :] + nxt * cqw_ref[2:3, :] + cqb_ref[...]
    sq = _silu(cq)
    hd = dm // MLSTM_HEADS
    q_ref[0] = sq[:, :dm].astype(BF16)
    kt_ref[0] = (sq[:, dm:] * (hd ** -0.5)).T.astype(BF16)
    v_ref[0] = un[:, 2 * dm:3 * dm].astype(BF16)
    og_ref[0] = jax.nn.sigmoid(un[:, 3 * dm:4 * dm])
    zg_ref[0] = _silu(un[:, 4 * dm:5 * dm]) * mg_ref[...]
    g_ref[0] = un[:, 5 * dm:] + bg_ref[...]

    uh = ut[:3 * dh]
    lg = lax.broadcasted_iota(jnp.int32, (1, tm), 1) % group
    prevh = jnp.where(lg == 0, 0.0, pltpu.roll(uh, 1, axis=1))
    nxth = jnp.where(lg == group - 1, 0.0, pltpu.roll(uh, tm - 1, axis=1))
    ch = prevh * chw_ref[:, 0:1] + uh * chw_ref[:, 1:2] + nxth * chw_ref[:, 2:3] + chb_ref[...]
    full = jnp.concatenate([ch, ut[3 * dh:]], axis=0)
    for tb in range(tm // TBLK):
        for half in range(TBLK // LANES):
            lo = tb * TBLK + half * LANES
            hy_ref[:, half, tb, 0] = full[:, lo:lo + LANES].reshape(4 * dh // CONV_CB, CONV_CB, LANES)


def _inproj(x, scale, shift, pre_gain, wn, wt, cqw, cqb, chw, chb, bg, mh_gain, *, group):
    bsz, seq, d = x.shape
    tm = min(seq, INPROJ_ROWS)
    nblk = seq // TBLK
    dm = (wn.shape[1] - GATE_PAD) // 5
    dh = wt.shape[0] // 4
    row = lambda b, t: (b, t, 0)
    const2 = lambda b, t: (0, 0)
    kern = functools.partial(_inproj_kernel, group=group, dm=dm, dh=dh)
    return pl.pallas_call(
        kern,
        grid=(bsz, seq // tm),
        in_specs=[
            pl.BlockSpec((1, tm, d), row),
            pl.BlockSpec((1, 1, d), lambda b, t: (b, 0, 0)),
            pl.BlockSpec((1, 1, d), lambda b, t: (b, 0, 0)),
            pl.BlockSpec((1, d), const2),
            pl.BlockSpec(wn.shape, const2, pipeline_mode=pl.Buffered(1)),
            pl.BlockSpec(wt.shape, const2, pipeline_mode=pl.Buffered(1)),
            pl.BlockSpec(cqw.shape, const2),
            pl.BlockSpec(cqb.shape, const2),
            pl.BlockSpec(chw.shape, const2),
            pl.BlockSpec(chb.shape, const2),
            pl.BlockSpec(bg.shape, const2),
            pl.BlockSpec(mh_gain.shape, const2),
        ],
        out_specs=[
            pl.BlockSpec((1, tm, dm), row),
            pl.BlockSpec((1, dm, tm), lambda b, t: (b, 0, t)),
            pl.BlockSpec((1, tm, dm), row),
            pl.BlockSpec((1, tm, dm), row),
            pl.BlockSpec((1, tm, dm), row),
            pl.BlockSpec((1, tm, GATE_PAD), row),
            pl.BlockSpec((4 * dh // CONV_CB, TBLK // LANES, tm // TBLK, 1, CONV_CB, LANES),
                         lambda b, t: (0, 0, t, b, 0, 0)),
        ],
        out_shape=[
            jax.ShapeDtypeStruct((bsz, seq, dm), BF16),
            jax.ShapeDtypeStruct((bsz, dm, seq), BF16),
            jax.ShapeDtypeStruct((bsz, seq, dm), BF16),
            jax.ShapeDtypeStruct((bsz, seq, dm), F32),
            jax.ShapeDtypeStruct((bsz, seq, dm), F32),
            jax.ShapeDtypeStruct((bsz, seq, GATE_PAD), F32),
            jax.ShapeDtypeStruct((4 * dh // CONV_CB, TBLK // LANES, nblk, bsz, CONV_CB, LANES), F32),
        ],
        compiler_params=_params("arbitrary", "arbitrary"),
        name="inproj",
    )(x, scale, shift, pre_gain, wn, wt, cqw, cqb, chw, chb, bg, mh_gain)


def _mlstm_kernel(q_ref, kt_ref, v_ref, g_ref, og_ref, zg_ref, c0_ref, m0_ref,
                  out_ref, cf_ref, mf_ref, hb_ref, c_ref, m_ref, *, tb, lc, nc, nh, dh):
    phase = pl.program_id(1)
    j = pl.program_id(2)

    @pl.when((phase == 0) & (j == 0))
    def _():
        m_ref[...] = m0_ref[0]

    def run(rev):
        d = 1 if rev else 0

        @pl.when(j == 0)
        def _():
            c_ref[...] = c0_ref[0, d]

        row0 = pl.multiple_of(((nc - 1 - j) if rev else j) * tb, tb)
        r = lax.broadcasted_iota(jnp.int32, (lc, lc), 0)
        s = lax.broadcasted_iota(jnp.int32, (lc, lc), 1)
        causal = (s >= r) if rev else (s <= r)
        tri = causal.astype(BF16)
        ones = jnp.ones((lc, dh), BF16)
        lanes = lax.broadcasted_iota(jnp.int32, (1, GATE_PAD), 1)
        mine = (lanes >= 2 * nh * d + nh) & (lanes < 2 * nh * (d + 1))
        subs = list(range(tb // lc))
        order = subs[::-1] if rev else subs
        pick1 = lambda row, fc: jnp.sum(jnp.where(lanes == fc, row, 0.0), axis=1, keepdims=True)

        prep = {}
        mrow = m_ref[...]
        for sc in order:
            rs = slice(sc * lc, (sc + 1) * lc)
            g = g_ref[0, rs, :] * LOG2E
            lf = jnp.minimum(g, 0.0) - jnp.log2(1.0 + jnp.exp2(-jnp.abs(g)))
            hi = lf.astype(BF16)
            rem = lf - hi.astype(F32)
            mid = rem.astype(BF16)
            lo = (rem - mid.astype(F32)).astype(BF16)
            bcum = (jnp.dot(tri, hi, preferred_element_type=F32) + jnp.dot(tri, mid, preferred_element_type=F32)
                    + jnp.dot(tri, lo, preferred_element_type=F32))
            a = pltpu.roll(g, nh, axis=1) - bcum
            amt = jnp.maximum(jnp.max(a, axis=0, keepdims=True), mrow)
            prep[sc] = (bcum, a.T, mrow, amt, jnp.exp2(mrow - amt))
            mrow = jnp.where(mine, jnp.sum(lf, axis=0, keepdims=True) + amt, mrow)
        m_ref[...] = mrow

        wts = {}
        for sc in order:
            bcum, at, mrow_c, amt, wcrow = prep[sc]
            for h in range(nh):
                fc = 2 * nh * d + nh + h
                mprev = pick1(mrow_c, fc)
                arow = at[fc:fc + 1, :]
                am = jnp.where(causal, arow, -jnp.inf)
                mtb = jnp.broadcast_to(jnp.maximum(jnp.max(am, axis=1, keepdims=True), mprev), (lc, dh))
                bcol = jnp.broadcast_to(bcum[:, fc:fc + 1], (lc, dh))
                wts[sc, h] = (jnp.exp2(am - mtb), jnp.exp2(mprev - mtb), jnp.exp2(-(bcol + mtb)),
                              jnp.exp2(arow - pick1(amt, fc)), pick1(wcrow, fc))

        for sc in order:
            rs = slice(sc * lc, (sc + 1) * lc)
            rows = pl.ds(row0 + sc * lc, lc)
            for h in range(nh):
                hs = slice(h * dh, (h + 1) * dh)
                e, wq, emt, wsrow, wc = wts[sc, h]
                qh = q_ref[0, rs, hs]
                kth = kt_ref[0, hs, rs]
                vaug = jnp.concatenate([v_ref[0, rs, hs], ones], axis=1)
                caug = c_ref[h]
                sm = jnp.dot(qh, kth, preferred_element_type=F32) * e
                qw = qh.astype(F32) * wq
                lhs = jnp.concatenate([sm.astype(BF16), qw.astype(BF16)], axis=1)
                rhs = jnp.concatenate([vaug, caug.astype(BF16)], axis=0)
                nd = jnp.dot(lhs, rhs, preferred_element_type=F32)
                hh = nd[:, :dh] / jnp.maximum(jnp.abs(nd[:, dh:]), emt)
                kwt = (kth.astype(F32) * wsrow).astype(BF16)
                c_ref[h] = wc * caug + jnp.dot(kwt, vaug, preferred_element_type=F32)
                if rev:
                    hb_ref[rows, hs] = hh
                else:
                    ho = (hh + hb_ref[rows, hs]) * og_ref[0, rs, hs]
                    ho = ho * lax.rsqrt(jnp.mean(ho * ho, axis=-1, keepdims=True) + RMS_EPS)
                    out_ref[0, rs, hs] = ho * zg_ref[0, rs, hs]

        @pl.when(j == nc - 1)
        def _():
            cf_ref[0, d] = c_ref[...]
            if not rev:
                mf_ref[0] = m_ref[...]

    pl.when(phase == 0)(functools.partial(run, True))
    pl.when(phase == 1)(functools.partial(run, False))


def _mlstm(q, kt, v, g, og, zg, c0, m0):
    bsz, seq, dmw = q.shape
    nh = MLSTM_HEADS
    dh = dmw // nh
    tb = min(seq, MLSTM_BLOCK)
    nc = seq // tb
    chunk = lambda b, p, j: (b, jnp.where(p == 0, nc - 1 - j, j), 0)
    chunk_t = lambda b, p, j: (b, 0, jnp.where(p == 0, nc - 1 - j, j))
    fwd_only = lambda b, p, j: (b, j * p, 0)
    st5 = lambda b, p, j: (b, 0, 0, 0, 0)
    st3 = lambda b, p, j: (b, 0, 0)
    kern = functools.partial(_mlstm_kernel, tb=tb, lc=MLSTM_CHUNK, nc=nc, nh=nh, dh=dh)
    return pl.pallas_call(
        kern,
        grid=(bsz, 2, nc),
        in_specs=[
            pl.BlockSpec((1, tb, dmw), chunk),
            pl.BlockSpec((1, dmw, tb), chunk_t),
            pl.BlockSpec((1, tb, dmw), chunk),
            pl.BlockSpec((1, tb, GATE_PAD), chunk),
            pl.BlockSpec((1, tb, dmw), fwd_only),
            pl.BlockSpec((1, tb, dmw), fwd_only),
            pl.BlockSpec((1, 2, nh, dh, 2 * dh), st5),
            pl.BlockSpec((1, 1, GATE_PAD), st3),
        ],
        out_specs=[
            pl.BlockSpec((1, tb, dmw), fwd_only),
            pl.BlockSpec((1, 2, nh, dh, 2 * dh), st5),
            pl.BlockSpec((1, 1, GATE_PAD), st3),
        ],
        out_shape=[
            jax.ShapeDtypeStruct((bsz, seq, dmw), F32),
            jax.ShapeDtypeStruct((bsz, 2, nh, dh, 2 * dh), F32),
            jax.ShapeDtypeStruct((bsz, 1, GATE_PAD), F32),
        ],
        scratch_shapes=[
            pltpu.VMEM((seq, dmw), F32),
            pltpu.VMEM((nh, dh, 2 * dh), F32),
            pltpu.VMEM((1, GATE_PAD), F32),
        ],
        compiler_params=_params("arbitrary", "arbitrary", "arbitrary"),
        name="mlstm",
    )(q, kt, v, g, og, zg, c0, m0)


def _conv_geometry(seq):
    nblk = seq // TBLK
    dmax = min(nblk - 1, (seq // 2 + TBLK - 2) // TBLK)
    ncols = (2 * dmax + 1) * TBLK
    n_base = seq // 2 + TBLK * (1 - dmax)
    lpad = n_base + ncols
    return nblk, dmax, ncols, n_base, lpad


def _filter_kernel(w1_ref, b1_ref, fr_ref, w2_ref, b2_ref, w3_ref, rate_ref, o_ref, h2_ref, *, seq, lpad):
    pos = lax.broadcasted_iota(jnp.int32, (1, seq), 1).astype(F32)

    @pl.when(pl.program_id(0) == 0)
    def _():
        bands = (lax.broadcasted_iota(jnp.int32, (FILTER_BANDS, 1), 0) + 1).astype(F32)
        ang = (2.0 * math.pi / seq) * pos * bands
        sn = jnp.sin(ang)
        cs = jnp.cos(ang)
        pre = b1_ref[...] + w1_ref[:, 0:1] * (pos / seq)
        for kb in range(FILTER_BANDS):
            pre = pre + w1_ref[:, 1 + kb:2 + kb] * sn[kb:kb + 1, :]
            pre = pre + w1_ref[:, 1 + FILTER_BANDS + kb:2 + FILTER_BANDS + kb] * cs[kb:kb + 1, :]
        h1 = jnp.sin(fr_ref[:, 0:1] * pre)
        h2_ref[...] = jnp.sin(fr_ref[:, 1:2] * (
            jnp.dot(w2_ref[...], h1, precision=HI, preferred_element_type=F32) + b2_ref[...]))

    f = jnp.dot(w3_ref[...], h2_ref[...], precision=HI, preferred_element_type=F32)
    offset = jnp.abs(pos - (seq // 2)) / (seq / 2)
    f = f * jnp.exp(-offset * rate_ref[...])
    f = f * lax.rsqrt(jnp.sum(f * f, axis=1, keepdims=True) + RMS_EPS)
    rows = f.shape[0]
    o_ref[:, :TBLK] = jnp.zeros((rows, TBLK), F32)
    o_ref[:, TBLK:TBLK + seq] = f
    o_ref[:, TBLK + seq:] = jnp.zeros((rows, lpad - TBLK - seq), F32)


def _filters(seq, w1, b1, freq, w2, b2, w3):
    hidden = w2.shape[0]
    nrow = w3.shape[1]
    c = nrow // HYENA_ORDER
    _, _, _, _, lpad = _conv_geometry(seq)
    rates = jnp.abs(jnp.linspace(MIN_DECAY, MAX_DECAY, c, dtype=F32))
    rates = jnp.tile(rates, HYENA_ORDER).reshape(nrow, 1)
    rb = LANES
    const = lambda i: (0, 0)
    out = pl.pallas_call(
        functools.partial(_filter_kernel, seq=seq, lpad=lpad),
        grid=(nrow // rb,),
        in_specs=[
            pl.BlockSpec((hidden, w1.shape[0]), const),
            pl.BlockSpec((hidden, 1), const),
            pl.BlockSpec((hidden, 2), const),
            pl.BlockSpec((hidden, hidden), const),
            pl.BlockSpec((hidden, 1), const),
            pl.BlockSpec((rb, hidden), lambda i: (i, 0)),
            pl.BlockSpec((rb, 1), lambda i: (i, 0)),
        ],
        out_specs=pl.BlockSpec((rb, lpad), lambda i: (i, 0)),
        out_shape=jax.ShapeDtypeStruct((nrow, lpad), F32),
        scratch_shapes=[pltpu.VMEM((hidden, seq), F32)],
        compiler_params=_params("arbitrary"),
        name="hyena_filter",
    )(w1.T, b1.reshape(hidden, 1), freq.T, w2.T, b2.reshape(hidden, 1), w3.T, rates)
    return out.reshape(nrow, lpad // LANES, LANES)


def _longconv_kernel(*refs, m, mp, pad, bsz, cb, dmax, nj, jt0, nb, with_zh):
    if with_zh:
        fb_ref, u_ref, x_ref, zh_ref, f_ref, o_ref, p_ref, a_ref, uf_ref = refs
    else:
        fb_ref, u_ref, x_ref, f_ref, o_ref, p_ref, a_ref, uf_ref = refs
        zh_ref = None
    mc = m * cb
    base = pl.program_id(0) * cb
    p_ref[...] = jnp.zeros(p_ref.shape, BF16)
    lane = lax.broadcasted_iota(jnp.int32, (LANES, LANES), 1)
    sub = lax.broadcasted_iota(jnp.int32, (LANES, LANES), 0)
    keep = lane >= sub

    def rows(ref, c):
        return jnp.concatenate([ref[0, pl.ds(hf * mc + c, m, stride=cb), :] for hf in range(TBLK // LANES)], axis=1)

    def produce(c, slot):
        u = rows(u_ref, c)
        uf_ref[slot] = u
        p_ref[slot, 0, pad:pad + m, :] = u.astype(BF16)
        p_ref[slot, 1, pad:pad + bsz + m, :] = jnp.concatenate([jnp.zeros((bsz, TBLK), F32), u], axis=0).astype(BF16)
        prev = None
        for jj in range(jt0 - 2, jt0 + nb):
            tap = jnp.broadcast_to(f_ref[0, pl.ds(c * nj + jj, 1), :], (LANES, LANES))
            cur = pltpu.roll(tap, 0, axis=1, stride=1, stride_axis=0).astype(BF16)
            if prev is not None:
                blk = jnp.where(keep, cur, prev)
                if jj >= jt0:
                    a_ref[slot, :LANES, (jj - jt0) * LANES:(jj - jt0 + 1) * LANES] = blk
                if jj + 1 < jt0 + nb:
                    a_ref[slot, LANES:, (jj + 1 - jt0) * LANES:(jj + 2 - jt0) * LANES] = blk
            prev = cur

    def consume(c, slot):
        acc = jnp.zeros((mp, TBLK), F32)
        for d in range(-dmax, dmax + 1):
            start = pad - bsz * (d - d % 2)
            lhs = p_ref[slot, d % 2, start:start + mp, :]
            col = (d + dmax) * TBLK
            acc = acc + jnp.dot(lhs, a_ref[slot, :, col:col + TBLK], preferred_element_type=F32)
        y = rows(x_ref, c) * (acc[:m] + uf_ref[slot] * fb_ref[base + c])
        if with_zh:
            y = y * _silu(rows(zh_ref, c))
        for hf in range(TBLK // LANES):
            o_ref[0, pl.ds(hf * mc + c, m, stride=cb), :] = y[:, hf * LANES:(hf + 1) * LANES]

    def pair(i, carry):
        c = 2 * i
        produce(c + 1, 1)
        consume(c, 0)
        produce(jnp.minimum(c + 2, cb - 1), 0)
        consume(c + 1, 1)
        return carry

    produce(0, 0)
    lax.fori_loop(0, cb // 2, pair, 0)


def _longconv(u, u_off, x, x_off, zh, zh_off, filt, f_off, fbias, seq):
    _, nhalf, nblk, bsz, cb, _ = u.shape
    c = fbias.shape[0]
    _, dmax, ncols, n_base, lpad = _conv_geometry(seq)
    nj = lpad // LANES
    m = nblk * bsz
    mp = max(m, 2 * SUBLANES)
    pad = -(-bsz * dmax // BF16_ROWS) * BF16_ROWS
    with_zh = zh is not None
    rows = nhalf * m * cb
    flat = lambda a: a.reshape(a.shape[0], rows, LANES)
    spec = lambda off: pl.BlockSpec((1, rows, LANES), lambda i: (off // cb + i, 0, 0))
    in_specs = [pl.BlockSpec(memory_space=pltpu.SMEM), spec(u_off), spec(x_off)]
    args = [fbias, flat(u), flat(x)]
    if with_zh:
        in_specs.append(spec(zh_off))
        args.append(flat(zh))
    in_specs.append(pl.BlockSpec((1, cb * nj, LANES), lambda i: (f_off // cb + i, 0, 0)))
    args.append(filt.reshape(filt.shape[0] // cb, cb * nj, LANES))
    kern = functools.partial(_longconv_kernel, m=m, mp=mp, pad=pad, bsz=bsz, cb=cb, dmax=dmax, nj=nj,
                             jt0=n_base // LANES, nb=ncols // LANES, with_zh=with_zh)
    out = pl.pallas_call(
        kern,
        grid=(c // cb,),
        in_specs=in_specs,
        out_specs=pl.BlockSpec((1, rows, LANES), lambda i: (i, 0, 0)),
        out_shape=jax.ShapeDtypeStruct((c // cb, rows, LANES), F32),
        scratch_shapes=[
            pltpu.VMEM((2, 2, 2 * pad + bsz + mp, TBLK), BF16),
            pltpu.VMEM((2, TBLK, ncols), BF16),
            pltpu.VMEM((2, m, TBLK), F32),
        ],
        compiler_params=_params("arbitrary"),
        name="hyena_longconv",
    )(*args)
    return out.reshape(c // cb, nhalf, nblk, bsz, cb, LANES)


def _outproj_kernel(m_ref, yh_ref, x_ref, wa_ref, wb_ref, pg_ref, gate_ref, o_ref):
    mm = m_ref[0].astype(BF16)
    ncb, nhalf, _, _, cb, _ = yh_ref.shape
    yh = jnp.concatenate([yh_ref[:, hf, 0, 0].reshape(ncb * cb, LANES) for hf in range(nhalf)], axis=1)
    y = (jnp.dot(mm, wa_ref[...], preferred_element_type=F32)
         + jnp.dot(yh.T.astype(BF16), wb_ref[...], preferred_element_type=F32))
    r = y * lax.rsqrt(jnp.mean(y * y, axis=-1, keepdims=True) + RMS_EPS) * pg_ref[...]
    o_ref[0] = x_ref[0] + gate_ref[0] * r


def _outproj(mm, yh, x, wa, wb, post_gain, gate):
    bsz, seq, d = x.shape
    tm = TBLK
    dm = mm.shape[2]
    ncb, nhalf, _, _, cb, _ = yh.shape
    row = lambda b, t: (b, t, 0)
    const2 = lambda b, t: (0, 0)
    return pl.pallas_call(
        _outproj_kernel,
        grid=(bsz, seq // tm),
        in_specs=[
            pl.BlockSpec((1, tm, dm), row),
            pl.BlockSpec((ncb, nhalf, 1, 1, cb, LANES), lambda b, t: (0, 0, t, b, 0, 0)),
            pl.BlockSpec((1, tm, d), row),
            pl.BlockSpec(wa.shape, const2),
            pl.BlockSpec(wb.shape, const2),
            pl.BlockSpec((1, d), const2),
            pl.BlockSpec((1, 1, d), lambda b, t: (b, 0, 0)),
        ],
        out_specs=pl.BlockSpec((1, tm, d), row),
        out_shape=jax.ShapeDtypeStruct((bsz, seq, d), F32),
        compiler_params=_params("arbitrary", "arbitrary"),
        name="outproj",
    )(mm, yh, x, wa, wb, post_gain, gate)


def _mixer(x, scale, shift, gate, state, lw, *, group, need_out):
    seq = x.shape[1]
    q, kt, v, og, zg, g, hy = _inproj(x, scale, shift, lw["pre_gain"], lw["wn"], lw["wt"], lw["cqw"], lw["cqb"],
                                     lw["chw"], lw["chb"], lw["bg"], lw["mh_gain"], group=group)
    mm, cf, mf = _mlstm(q, kt, v, g, og, zg, *state)
    if not need_out:
        return None, (cf, mf)
    dh = lw["fbias"].shape[1]
    filt = _filters(seq, lw["fw1"], lw["fb1"], lw["ffreq"], lw["fw2"], lw["fb2"], lw["fw3"])
    z = _longconv(hy, 0, hy, dh, None, 0, filt, 0, lw["fbias"][0], seq)
    yh = _longconv(z, 0, hy, 2 * dh, hy, 3 * dh, filt, dh, lw["fbias"][1], seq)
    new_x = _outproj(mm, yh, x, lw["wa"], lw["wb"], lw["post_gain"], gate)
    return new_x, (cf, mf)


def kernel(x, c, ctx, c_ctx, w_mod, b_mod, pre_gain, post_gain, w_in, b_gates, conv_qk_w, conv_qk_b, mh_gain,
           hy_conv_w, hy_conv_b, filt_w1, filt_b1, filt_freq, filt_w2, filt_b2, filt_w3, filt_bias, w_out):
    bsz, _, d = x.shape
    depth = w_mod.shape[0]
    dm = mh_gain.shape[1]
    dh = filt_bias.shape[2]
    nh = MLSTM_HEADS
    hd = dm // nh
    ngate = b_gates.shape[1]
    o_gate = 5 * dm
    o_hy = o_gate + ngate

    mod_rows = 2 * SUBLANES
    cc = jnp.concatenate([c, c_ctx[None, :], jnp.zeros((mod_rows - bsz - 1, d), F32)], axis=0)
    mod = _modulation(cc, w_mod, b_mod)

    zero_state = (jnp.zeros((bsz, 2, nh, hd, 2 * hd), F32), jnp.zeros((bsz, 1, GATE_PAD), F32))
    xl, xc = x, ctx
    for l in range(depth):
        last = l == depth - 1
        w_l = w_in[l].astype(BF16)
        gate_w = jnp.pad(w_l[:, o_gate:o_hy], ((0, 0), (0, GATE_PAD - ngate)))
        lw = dict(
            pre_gain=pre_gain[l].reshape(1, d), post_gain=post_gain[l].reshape(1, d),
            wn=jnp.concatenate([w_l[:, :o_gate], gate_w], axis=1), wt=w_l[:, o_hy:].T,
            cqw=conv_qk_w[l], cqb=conv_qk_b[l].reshape(1, 2 * dm),
            chw=hy_conv_w[l].T, chb=hy_conv_b[l].reshape(3 * dh, 1),
            bg=jnp.pad(b_gates[l], (0, GATE_PAD - ngate)).reshape(1, GATE_PAD),
            mh_gain=mh_gain[l].reshape(1, dm),
            fw1=filt_w1[l], fb1=filt_b1[l], ffreq=filt_freq[l], fw2=filt_w2[l], fb2=filt_b2[l], fw3=filt_w3[l],
            fbias=filt_bias[l], wa=w_out[l, :dm].astype(BF16), wb=w_out[l, dm:].astype(BF16),
        )
        ml = mod[l]
        split3 = lambda a: (a[..., :d], a[..., d:2 * d], a[..., 2 * d:])
        shift_l, scale_l, gate_l = split3(ml[:bsz].reshape(bsz, 1, 3 * d))
        shift_c, scale_c, gate_c = split3(jnp.broadcast_to(ml[bsz].reshape(1, 1, 3 * d), (bsz, 1, 3 * d)))
        new_xc, ctx_state = _mixer(xc, scale_c, shift_c, gate_c, zero_state, lw, group=xc.shape[1], need_out=not last)
        xl, _ = _mixer(xl, scale_l, shift_l, gate_l, ctx_state, lw, group=GRID_W, need_out=True)
        if not last:
            xc = new_xc
    return xl
```

```python
import functools
import math

import jax
import jax.numpy as jnp
from jax import lax
from jax.experimental import pallas as pl
from jax.experimental.pallas import tpu as pltpu

F32 = jnp.float32
BF16 = jnp.bfloat16
HI = lax.Precision.HIGHEST

LANES = 128
SUBLANES = 8
BF16_ROWS = 2 * SUBLANES
MXU_DIM = 256
VMEM_LIMIT = 56 * 1024 * 1024

GRID_W = 64
MLSTM_HEADS = 4
HYENA_ORDER = 2
FILTER_BANDS = 8
DECAY_TARGET = 1e-2
FAST_DECAY_PCT = 0.3
SLOW_DECAY_PCT = 1.5
MIN_DECAY = math.log(DECAY_TARGET) / SLOW_DECAY_PCT
MAX_DECAY = math.log(DECAY_TARGET) / FAST_DECAY_PCT
RMS_EPS = 1e-6

TBLK = MXU_DIM
CONV_CB = 2 * SUBLANES
CONV_DEPTH = 3
INPROJ_ROWS = 2 * TBLK
GATE_PAD = LANES
MLSTM_CHUNK = LANES
MLSTM_BLOCK = 4 * MLSTM_CHUNK
LOG2E = math.log2(math.e)
NT_DIMS = (((1,), (1,)), ((), ()))


def _params(*sem):
    return pltpu.CompilerParams(dimension_semantics=sem, vmem_limit_bytes=VMEM_LIMIT)


def _silu(x):
    return x * jax.nn.sigmoid(x)


def _mod_kernel(c_ref, w_ref, b_ref, o_ref):
    s = _silu(c_ref[...])
    o_ref[0] = jnp.dot(s, w_ref[0], precision=HI, preferred_element_type=F32) + b_ref[0]


def _modulation(cc, w_mod, b_mod):
    depth, d, d3 = w_mod.shape
    rows = cc.shape[0]
    nb = d3 // d
    return pl.pallas_call(
        _mod_kernel,
        grid=(depth, nb),
        in_specs=[
            pl.BlockSpec((rows, d), lambda l, n: (0, 0)),
            pl.BlockSpec((1, d, d), lambda l, n: (l, 0, n)),
            pl.BlockSpec((1, 1, d), lambda l, n: (l, 0, n)),
        ],
        out_specs=pl.BlockSpec((1, rows, d), lambda l, n: (l, 0, n)),
        out_shape=jax.ShapeDtypeStruct((depth, rows, d3), F32),
        compiler_params=_params("arbitrary", "arbitrary"),
        name="modulation",
    )(cc, w_mod, b_mod.reshape(depth, 1, d3))


def _inproj_kernel(x_ref, sc_ref, sh_ref, pg_ref, wn_ref, wt_ref, cqw_ref, cqb_ref, chw_ref, chb_ref, bg_ref,
                   mg_ref, q_ref, kt_ref, v_ref, og_ref, zg_ref, g_ref, hy_ref, *, group, dm, dh):
    tm = x_ref.shape[1]
    ts = min(tm, TBLK)
    subs = [slice(i * ts, (i + 1) * ts) for i in range(tm // ts)]
    hs = []
    for rs in subs:
        x = x_ref[0, rs, :]
        ms = jnp.mean(x * x, axis=-1, keepdims=True)
        y = x * lax.rsqrt(ms + RMS_EPS) * pg_ref[...]
        hs.append((y * (1.0 + sc_ref[0]) + sh_ref[0]).astype(BF16))
    hd = dm // MLSTM_HEADS
    proj = lambda h, lo, hi: jnp.dot(h, wn_ref[:, lo:hi], preferred_element_type=F32)

    tg = lax.broadcasted_iota(jnp.int32, (ts, 1), 0) % group
    for part in range(2):
        cs = slice(part * dm, (part + 1) * dm)
        for rs, h in zip(subs, hs):
            uqk = proj(h, part * dm, (part + 1) * dm)
            prev = jnp.where(tg == 0, 0.0, pltpu.roll(uqk, 1, axis=0))
            nxt = jnp.where(tg == group - 1, 0.0, pltpu.roll(uqk, ts - 1, axis=0))
            sq = _silu(prev * cqw_ref[0:1, cs] + uqk * cqw_ref[1:2, cs] + nxt * cqw_ref[2:3, cs] + cqb_ref[:, cs])
            if part == 0:
                q_ref[0, rs, :] = sq.astype(BF16)
            else:
                kt_ref[0, :, rs] = (sq * (hd ** -0.5)).T.astype(BF16)
    for rs, h in zip(subs, hs):
        v_ref[0, rs, :] = proj(h, 2 * dm, 3 * dm).astype(BF16)
    for rs, h in zip(subs, hs):
        og_ref[0, rs, :] = jax.nn.sigmoid(proj(h, 3 * dm, 4 * dm))
    for rs, h in zip(subs, hs):
        zg_ref[0, rs, :] = _silu(proj(h, 4 * dm, 5 * dm)) * mg_ref[...]
    for rs, h in zip(subs, hs):
        g_ref[0, rs, :] = proj(h, 5 * dm, 5 * dm + GATE_PAD) + bg_ref[...]

    lg = lax.broadcasted_iota(jnp.int32, (1, ts), 1) % group
    ncb = dh // CONV_CB
    for part in range(4):
        ps = slice(part * dh, (part + 1) * dh)
        for tb, h in enumerate(hs):
            uh = lax.dot_general(wt_ref[ps, :], h, NT_DIMS, preferred_element_type=F32)
            if part < 3:
                prevh = jnp.where(lg == 0, 0.0, pltpu.roll(uh, 1, axis=1))
                nxth = jnp.where(lg == group - 1, 0.0, pltpu.roll(uh, ts - 1, axis=1))
                uh = prevh * chw_ref[ps, 0:1] + uh * chw_ref[ps, 1:2] + nxth * chw_ref[ps, 2:3] + chb_ref[ps, :]
            for half in range(ts // LANES):
                hy_ref[part * ncb:(part + 1) * ncb, half, tb, 0] = (
                    uh[:, half * LANES:(half + 1) * LANES].reshape(ncb, CONV_CB, LANES))


def _inproj(x, scale, shift, pre_gain, wn, wt, cqw, cqb, chw, chb, bg, mh_gain, *, group):
    bsz, seq, d = x.shape
    tm = min(seq, INPROJ_ROWS)
    nblk = seq // TBLK
    dm = (wn.shape[1] - GATE_PAD) // 5
    dh = wt.shape[0] // 4
    row = lambda b, t: (b, t, 0)
    const2 = lambda b, t: (0, 0)
    kern = functools.partial(_inproj_kernel, group=group, dm=dm, dh=dh)
    return pl.pallas_call(
        kern,
        grid=(bsz, seq // tm),
        in_specs=[
            pl.BlockSpec((1, tm, d), row),
            pl.BlockSpec((1, 1, d), lambda b, t: (b, 0, 0)),
            pl.BlockSpec((1, 1, d), lambda b, t: (b, 0, 0)),
            pl.BlockSpec((1, d), const2),
            pl.BlockSpec(wn.shape, const2, pipeline_mode=pl.Buffered(1)),
            pl.BlockSpec(wt.shape, const2, pipeline_mode=pl.Buffered(1)),
            pl.BlockSpec(cqw.shape, const2),
            pl.BlockSpec(cqb.shape, const2),
            pl.BlockSpec(chw.shape, const2),
            pl.BlockSpec(chb.shape, const2),
            pl.BlockSpec(bg.shape, const2),
            pl.BlockSpec(mh_gain.shape, const2),
        ],
        out_specs=[
            pl.BlockSpec((1, tm, dm), row),
            pl.BlockSpec((1, dm, tm), lambda b, t: (b, 0, t)),
            pl.BlockSpec((1, tm, dm), row),
            pl.BlockSpec((1, tm, dm), row),
            pl.BlockSpec((1, tm, dm), row),
            pl.BlockSpec((1, tm, GATE_PAD), row),
            pl.BlockSpec((4 * dh // CONV_CB, TBLK // LANES, tm // TBLK, 1, CONV_CB, LANES),
                         lambda b, t: (0, 0, t, b, 0, 0)),
        ],
        out_shape=[
            jax.ShapeDtypeStruct((bsz, seq, dm), BF16),
            jax.ShapeDtypeStruct((bsz, dm, seq), BF16),
            jax.ShapeDtypeStruct((bsz, seq, dm), BF16),
            jax.ShapeDtypeStruct((bsz, seq, dm), F32),
            jax.ShapeDtypeStruct((bsz, seq, dm), F32),
            jax.ShapeDtypeStruct((bsz, seq, GATE_PAD), F32),
            jax.ShapeDtypeStruct((4 * dh // CONV_CB, TBLK // LANES, nblk, bsz, CONV_CB, LANES), F32),
        ],
        compiler_params=_params("arbitrary", "arbitrary"),
        name="inproj",
    )(x, scale, shift, pre_gain, wn, wt, cqw, cqb, chw, chb, bg, mh_gain)


def _mlstm_kernel(q_ref, kt_ref, v_ref, g_ref, og_ref, zg_ref, c0_ref, m0_ref,
                  out_ref, cf_ref, mf_ref, hb_ref, c_ref, m_ref, *, tb, lc, nc, nh, dh):
    phase = pl.program_id(1)
    j = pl.program_id(2)

    @pl.when((phase == 0) & (j == 0))
    def _():
        m_ref[...] = m0_ref[0]

    def run(rev):
        d = 1 if rev else 0

        @pl.when(j == 0)
        def _():
            c_ref[...] = c0_ref[0, d]

        row0 = pl.multiple_of(((nc - 1 - j) if rev else j) * tb, tb)
        r = lax.broadcasted_iota(jnp.int32, (lc, lc), 0)
        s = lax.broadcasted_iota(jnp.int32, (lc, lc), 1)
        causal = (s >= r) if rev else (s <= r)
        tri = causal.astype(BF16)
        ones = jnp.ones((lc, dh), BF16)
        lanes = lax.broadcasted_iota(jnp.int32, (1, GATE_PAD), 1)
        mine = (lanes >= 2 * nh * d + nh) & (lanes < 2 * nh * (d + 1))
        subs = list(range(tb // lc))
        order = subs[::-1] if rev else subs
        pick1 = lambda row, fc: jnp.sum(jnp.where(lanes == fc, row, 0.0), axis=1, keepdims=True)

        prep = {}
        mrow = m_ref[...]
        for sc in order:
            rs = slice(sc * lc, (sc + 1) * lc)
            g = g_ref[0, rs, :] * LOG2E
            lf = jnp.minimum(g, 0.0) - jnp.log2(1.0 + jnp.exp2(-jnp.abs(g)))
            hi = lf.astype(BF16)
            rem = lf - hi.astype(F32)
            mid = rem.astype(BF16)
            lo = (rem - mid.astype(F32)).astype(BF16)
            bcum = (jnp.dot(tri, hi, preferred_element_type=F32) + jnp.dot(tri, mid, preferred_element_type=F32)
                    + jnp.dot(tri, lo, preferred_element_type=F32))
            a = pltpu.roll(g, nh, axis=1) - bcum
            amt = jnp.maximum(jnp.max(a, axis=0, keepdims=True), mrow)
            prep[sc] = (bcum, a.T, mrow, amt, jnp.exp2(mrow - amt))
            mrow = jnp.where(mine, jnp.sum(lf, axis=0, keepdims=True) + amt, mrow)
        m_ref[...] = mrow

        wts = {}
        for sc in order:
            bcum, at, mrow_c, amt, wcrow = prep[sc]
            for h in range(nh):
                fc = 2 * nh * d + nh + h
                mprev = pick1(mrow_c, fc)
                arow = at[fc:fc + 1, :]
                am = jnp.where(causal, arow, -jnp.inf)
                mtb = jnp.broadcast_to(jnp.maximum(jnp.max(am, axis=1, keepdims=True), mprev), (lc, dh))
                bcol = jnp.broadcast_to(bcum[:, fc:fc + 1], (lc, dh))
                wts[sc, h] = (jnp.exp2(am - mtb), jnp.exp2(mprev - mtb), jnp.exp2(-(bcol + mtb)),
                              jnp.exp2(arow - pick1(amt, fc)), pick1(wcrow, fc))

        for sc in order:
            rs = slice(sc * lc, (sc + 1) * lc)
            rows = pl.ds(row0 + sc * lc, lc)
            for h in range(nh):
                hs = slice(h * dh, (h + 1) * dh)
                e, wq, emt, wsrow, wc = wts[sc, h]
                qh = q_ref[0, rs, hs]
                kth = kt_ref[0, hs, rs]
                vaug = jnp.concatenate([v_ref[0, rs, hs], ones], axis=1)
                caug = c_ref[h]
                sm = jnp.dot(qh, kth, preferred_element_type=F32) * e
                qw = qh.astype(F32) * wq
                lhs = jnp.concatenate([sm.astype(BF16), qw.astype(BF16)], axis=1)
                rhs = jnp.concatenate([vaug, caug.astype(BF16)], axis=0)
                nd = jnp.dot(lhs, rhs, preferred_element_type=F32)
                hh = nd[:, :dh] / jnp.maximum(jnp.abs(nd[:, dh:]), emt)
                kwt = (kth.astype(F32) * wsrow).astype(BF16)
                c_ref[h] = wc * caug + jnp.dot(kwt, vaug, preferred_element_type=F32)
                if rev:
                    hb_ref[rows, hs] = hh
                else:
                    ho = (hh + hb_ref[rows, hs]) * og_ref[0, rs, hs]
                    ho = ho * lax.rsqrt(jnp.mean(ho * ho, axis=-1, keepdims=True) + RMS_EPS)
                    out_ref[0, rs, hs] = ho * zg_ref[0, rs, hs]

        @pl.when(j == nc - 1)
        def _():
            cf_ref[0, d] = c_ref[...]
            if not rev:
                mf_ref[0] = m_ref[...]

    pl.when(phase == 0)(functools.partial(run, True))
    pl.when(phase == 1)(functools.partial(run, False))


def _mlstm(q, kt, v, g, og, zg, c0, m0):
    bsz, seq, dmw = q.shape
    nh = MLSTM_HEADS
    dh = dmw // nh
    tb = min(seq, MLSTM_BLOCK)
    nc = seq // tb
    chunk = lambda b, p, j: (b, jnp.where(p == 0, nc - 1 - j, j), 0)
    chunk_t = lambda b, p, j: (b, 0, jnp.where(p == 0, nc - 1 - j, j))
    fwd_only = lambda b, p, j: (b, j * p, 0)
    st5 = lambda b, p, j: (b, 0, 0, 0, 0)
    st3 = lambda b, p, j: (b, 0, 0)
    kern = functools.partial(_mlstm_kernel, tb=tb, lc=MLSTM_CHUNK, nc=nc, nh=nh, dh=dh)
    return pl.pallas_call(
        kern,
        grid=(bsz, 2, nc),
        in_specs=[
            pl.BlockSpec((1, tb, dmw), chunk),
            pl.BlockSpec((1, dmw, tb), chunk_t),
            pl.BlockSpec((1, tb, dmw), chunk),
            pl.BlockSpec((1, tb, GATE_PAD), chunk),
            pl.BlockSpec((1, tb, dmw), fwd_only),
            pl.BlockSpec((1, tb, dmw), fwd_only),
            pl.BlockSpec((1, 2, nh, dh, 2 * dh), st5),
            pl.BlockSpec((1, 1, GATE_PAD), st3),
        ],
        out_specs=[
            pl.BlockSpec((1, tb, dmw), fwd_only),
            pl.BlockSpec((1, 2, nh, dh, 2 * dh), st5),
            pl.BlockSpec((1, 1, GATE_PAD), st3),
        ],
        out_shape=[
            jax.ShapeDtypeStruct((bsz, seq, dmw), F32),
            jax.ShapeDtypeStruct((bsz, 2, nh, dh, 2 * dh), F32),
            jax.ShapeDtypeStruct((bsz, 1, GATE_PAD), F32),
        ],
        scratch_shapes=[
            pltpu.VMEM((seq, dmw), F32),
            pltpu.VMEM((nh, dh, 2 * dh), F32),
            pltpu.VMEM((1, GATE_PAD), F32),
        ],
        compiler_params=_params("arbitrary", "arbitrary", "arbitrary"),
        name="mlstm",
    )(q, kt, v, g, og, zg, c0, m0)


def _conv_geometry(seq):
    nblk = seq // TBLK
    dmax = min(nblk - 1, (seq // 2 + TBLK - 2) // TBLK)
    ncols = (2 * dmax + 1) * TBLK
    n_base = seq // 2 + TBLK * (1 - dmax)
    lpad = n_base + ncols
    return nblk, dmax, ncols, n_base, lpad


def _filter_kernel(w1_ref, b1_ref, fr_ref, w2_ref, b2_ref, w3_ref, rate_ref, o_ref, h2_ref, *, seq, lpad):
    pos = lax.broadcasted_iota(jnp.int32, (1, seq), 1).astype(F32)

    @pl.when(pl.program_id(0) == 0)
    def _():
        bands = (lax.broadcasted_iota(jnp.int32, (FILTER_BANDS, 1), 0) + 1).astype(F32)
        ang = (2.0 * math.pi / seq) * pos * bands
        sn = jnp.sin(ang)
        cs = jnp.cos(ang)
        pre = b1_ref[...] + w1_ref[:, 0:1] * (pos / seq)
        for kb in range(FILTER_BANDS):
            pre = pre + w1_ref[:, 1 + kb:2 + kb] * sn[kb:kb + 1, :]
            pre = pre + w1_ref[:, 1 + FILTER_BANDS + kb:2 + FILTER_BANDS + kb] * cs[kb:kb + 1, :]
        h1 = jnp.sin(fr_ref[:, 0:1] * pre)
        h2_ref[...] = jnp.sin(fr_ref[:, 1:2] * (
            jnp.dot(w2_ref[...], h1, precision=HI, preferred_element_type=F32) + b2_ref[...]))

    f = jnp.dot(w3_ref[...], h2_ref[...], precision=HI, preferred_element_type=F32)
    offset = jnp.abs(pos - (seq // 2)) / (seq / 2)
    f = f * jnp.exp(-offset * rate_ref[...])
    f = f * lax.rsqrt(jnp.sum(f * f, axis=1, keepdims=True) + RMS_EPS)
    rows = f.shape[0]
    o_ref[:, :TBLK] = jnp.zeros((rows, TBLK), F32)
    o_ref[:, TBLK:TBLK + seq] = f
    o_ref[:, TBLK + seq:] = jnp.zeros((rows, lpad - TBLK - seq), F32)


def _filters(seq, w1, b1, freq, w2, b2, w3):
    hidden = w2.shape[0]
    nrow = w3.shape[1]
    c = nrow // HYENA_ORDER
    _, _, _, _, lpad = _conv_geometry(seq)
    rates = jnp.abs(jnp.linspace(MIN_DECAY, MAX_DECAY, c, dtype=F32))
    rates = jnp.tile(rates, HYENA_ORDER).reshape(nrow, 1)
    rb = LANES
    const = lambda i: (0, 0)
    out = pl.pallas_call(
        functools.partial(_filter_kernel, seq=seq, lpad=lpad),
        grid=(nrow // rb,),
        in_specs=[
            pl.BlockSpec((hidden, w1.shape[0]), const),
            pl.BlockSpec((hidden, 1), const),
            pl.BlockSpec((hidden, 2), const),
            pl.BlockSpec((hidden, hidden), const),
            pl.BlockSpec((hidden, 1), const),
            pl.BlockSpec((rb, hidden), lambda i: (i, 0)),
            pl.BlockSpec((rb, 1), lambda i: (i, 0)),
        ],
        out_specs=pl.BlockSpec((rb, lpad), lambda i: (i, 0)),
        out_shape=jax.ShapeDtypeStruct((nrow, lpad), F32),
        scratch_shapes=[pltpu.VMEM((hidden, seq), F32)],
        compiler_params=_params("arbitrary"),
        name="hyena_filter",
    )(w1.T, b1.reshape(hidden, 1), freq.T, w2.T, b2.reshape(hidden, 1), w3.T, rates)
    return out.reshape(nrow, lpad // LANES, LANES)


def _longconv_kernel(*refs, m, mp, pad, bsz, cb, dmax, nj, jt0, nb, with_zh):
    if with_zh:
        fb_ref, u_ref, x_ref, zh_ref, f_ref, o_ref, p_ref, a_ref, uf_ref, acc_ref = refs
    else:
        fb_ref, u_ref, x_ref, f_ref, o_ref, p_ref, a_ref, uf_ref, acc_ref = refs
        zh_ref = None
    mc = m * cb
    base = pl.program_id(0) * cb
    p_ref[...] = jnp.zeros(p_ref.shape, BF16)
    lane = lax.broadcasted_iota(jnp.int32, (LANES, LANES), 1)
    sub = lax.broadcasted_iota(jnp.int32, (LANES, LANES), 0)
    keep = lane >= sub

    def rows(ref, c):
        return jnp.concatenate([ref[0, pl.ds(hf * mc + c, m, stride=cb), :] for hf in range(TBLK // LANES)], axis=1)

    def stage(c, slot):
        u = rows(u_ref, c)
        uf_ref[slot] = u
        p_ref[slot, 0, pad:pad + m, :] = u.astype(BF16)
        p_ref[slot, 1, pad:pad + bsz + m, :] = jnp.concatenate([jnp.zeros((bsz, TBLK), F32), u], axis=0).astype(BF16)
        prev = None
        for jj in range(jt0 - 2, jt0 + nb):
            tap = jnp.broadcast_to(f_ref[0, pl.ds(c * nj + jj, 1), :], (LANES, LANES))
            cur = pltpu.roll(tap, 0, axis=1, stride=1, stride_axis=0).astype(BF16)
            if prev is not None:
                blk = jnp.where(keep, cur, prev)
                if jj >= jt0:
                    a_ref[slot, :LANES, (jj - jt0) * LANES:(jj - jt0 + 1) * LANES] = blk
                if jj + 1 < jt0 + nb:
                    a_ref[slot, LANES:, (jj + 1 - jt0) * LANES:(jj + 2 - jt0) * LANES] = blk
            prev = cur

    def convolve(slot):
        acc = jnp.zeros((mp, TBLK), F32)
        for d in range(-dmax, dmax + 1):
            start = pad - bsz * (d - d % 2)
            lhs = p_ref[slot, d % 2, start:start + mp, :]
            col = (d + dmax) * TBLK
            acc = acc + jnp.dot(lhs, a_ref[slot, :, col:col + TBLK], preferred_element_type=F32)
        acc_ref[slot] = acc[:m]

    def finish(c, slot):
        y = rows(x_ref, c) * (acc_ref[slot] + uf_ref[slot] * fb_ref[base + c])
        if with_zh:
            y = y * _silu(rows(zh_ref, c))
        for hf in range(TBLK // LANES):
            o_ref[0, pl.ds(hf * mc + c, m, stride=cb), :] = y[:, hf * LANES:(hf + 1) * LANES]

    stage(0, 0)
    stage(1, 1)
    convolve(0)
    for c in range(1, cb):
        finish(c - 1, (c - 1) % CONV_DEPTH)
        if c + 1 < cb:
            stage(c + 1, (c + 1) % CONV_DEPTH)
        convolve(c % CONV_DEPTH)
    finish(cb - 1, (cb - 1) % CONV_DEPTH)


def _longconv(u, u_off, x, x_off, zh, zh_off, filt, f_off, fbias, seq):
    _, nhalf, nblk, bsz, cb, _ = u.shape
    c = fbias.shape[0]
    _, dmax, ncols, n_base, lpad = _conv_geometry(seq)
    nj = lpad // LANES
    m = nblk * bsz
    mp = max(m, 2 * SUBLANES)
    pad = -(-bsz * dmax // BF16_ROWS) * BF16_ROWS
    with_zh = zh is not None
    rows = nhalf * m * cb
    flat = lambda a: a.reshape(a.shape[0], rows, LANES)
    spec = lambda off: pl.BlockSpec((1, rows, LANES), lambda i: (off // cb + i, 0, 0))
    in_specs = [pl.BlockSpec(memory_space=pltpu.SMEM), spec(u_off), spec(x_off)]
    args = [fbias, flat(u), flat(x)]
    if with_zh:
        in_specs.append(spec(zh_off))
        args.append(flat(zh))
    in_specs.append(pl.BlockSpec((1, cb * nj, LANES), lambda i: (f_off // cb + i, 0, 0)))
    args.append(filt.reshape(filt.shape[0] // cb, cb * nj, LANES))
    kern = functools.partial(_longconv_kernel, m=m, mp=mp, pad=pad, bsz=bsz, cb=cb, dmax=dmax, nj=nj,
                             jt0=n_base // LANES, nb=ncols // LANES, with_zh=with_zh)
    out = pl.pallas_call(
        kern,
        grid=(c // cb,),
        in_specs=in_specs,
        out_specs=pl.BlockSpec((1, rows, LANES), lambda i: (i, 0, 0)),
        out_shape=jax.ShapeDtypeStruct((c // cb, rows, LANES), F32),
        scratch_shapes=[
            pltpu.VMEM((CONV_DEPTH, 2, 2 * pad + bsz + mp, TBLK), BF16),
            pltpu.VMEM((CONV_DEPTH, TBLK, ncols), BF16),
            pltpu.VMEM((CONV_DEPTH, m, TBLK), F32),
            pltpu.VMEM((CONV_DEPTH, m, TBLK), F32),
        ],
        compiler_params=_params("arbitrary"),
        name="hyena_longconv",
    )(*args)
    return out.reshape(c // cb, nhalf, nblk, bsz, cb, LANES)


def _outproj_kernel(m_ref, yh_ref, x_ref, wa_ref, wb_ref, pg_ref, gate_ref, o_ref):
    mm = m_ref[0].astype(BF16)
    ncb, nhalf, _, _, cb, _ = yh_ref.shape
    yh = jnp.concatenate([yh_ref[:, hf, 0, 0].reshape(ncb * cb, LANES) for hf in range(nhalf)], axis=1)
    y = (jnp.dot(mm, wa_ref[...], preferred_element_type=F32)
         + jnp.dot(yh.T.astype(BF16), wb_ref[...], preferred_element_type=F32))
    r = y * lax.rsqrt(jnp.mean(y * y, axis=-1, keepdims=True) + RMS_EPS) * pg_ref[...]
    o_ref[0] = x_ref[0] + gate_ref[0] * r


def _outproj(mm, yh, x, wa, wb, post_gain, gate):
    bsz, seq, d = x.shape
    tm = TBLK
    dm = mm.shape[2]
    ncb, nhalf, _, _, cb, _ = yh.shape
    row = lambda b, t: (b, t, 0)
    const2 = lambda b, t: (0, 0)
    return pl.pallas_call(
        _outproj_kernel,
        grid=(bsz, seq // tm),
        in_specs=[
            pl.BlockSpec((1, tm, dm), row),
            pl.BlockSpec((ncb, nhalf, 1, 1, cb, LANES), lambda b, t: (0, 0, t, b, 0, 0)),
            pl.BlockSpec((1, tm, d), row),
            pl.BlockSpec(wa.shape, const2),
            pl.BlockSpec(wb.shape, const2),
            pl.BlockSpec((1, d), const2),
            pl.BlockSpec((1, 1, d), lambda b, t: (b, 0, 0)),
        ],
        out_specs=pl.BlockSpec((1, tm, d), row),
        out_shape=jax.ShapeDtypeStruct((bsz, seq, d), F32),
        compiler_params=_params("arbitrary", "arbitrary"),
        name="outproj",
    )(mm, yh, x, wa, wb, post_gain, gate)


def _mixer(x, scale, shift, gate, state, lw, *, group, need_out):
    seq = x.shape[1]
    q, kt, v, og, zg, g, hy = _inproj(x, scale, shift, lw["pre_gain"], lw["wn"], lw["wt"], lw["cqw"], lw["cqb"],
                                     lw["chw"], lw["chb"], lw["bg"], lw["mh_gain"], group=group)
    mm, cf, mf = _mlstm(q, kt, v, g, og, zg, *state)
    if not need_out:
        return None, (cf, mf)
    dh = lw["fbias"].shape[1]
    filt = _filters(seq, lw["fw1"], lw["fb1"], lw["ffreq"], lw["fw2"], lw["fb2"], lw["fw3"])
    z = _longconv(hy, 0, hy, dh, None, 0, filt, 0, lw["fbias"][0], seq)
    yh = _longconv(z, 0, hy, 2 * dh, hy, 3 * dh, filt, dh, lw["fbias"][1], seq)
    new_x = _outproj(mm, yh, x, lw["wa"], lw["wb"], lw["post_gain"], gate)
    return new_x, (cf, mf)


def kernel(x, c, ctx, c_ctx, w_mod, b_mod, pre_gain, post_gain, w_in, b_gates, conv_qk_w, conv_qk_b, mh_gain,
           hy_conv_w, hy_conv_b, filt_w1, filt_b1, filt_freq, filt_w2, filt_b2, filt_w3, filt_bias, w_out):
    bsz, _, d = x.shape
    depth = w_mod.shape[0]
    dm = mh_gain.shape[1]
    dh = filt_bias.shape[2]
    nh = MLSTM_HEADS
    hd = dm // nh
    ngate = b_gates.shape[1]
    o_gate = 5 * dm
    o_hy = o_gate + ngate

    mod_rows = 2 * SUBLANES
    cc = jnp.concatenate([c, c_ctx[None, :], jnp.zeros((mod_rows - bsz - 1, d), F32)], axis=0)
    mod = _modulation(cc, w_mod, b_mod)

    zero_state = (jnp.zeros((bsz, 2, nh, hd, 2 * hd), F32), jnp.zeros((bsz, 1, GATE_PAD), F32))
    xl, xc = x, ctx
    for l in range(depth):
        last = l == depth - 1
        w_l = w_in[l].astype(BF16)
        gate_w = jnp.pad(w_l[:, o_gate:o_hy], ((0, 0), (0, GATE_PAD - ngate)))
        lw = dict(
            pre_gain=pre_gain[l].reshape(1, d), post_gain=post_gain[l].reshape(1, d),
            wn=jnp.concatenate([w_l[:, :o_gate], gate_w], axis=1), wt=w_l[:, o_hy:].T,
            cqw=conv_qk_w[l], cqb=conv_qk_b[l].reshape(1, 2 * dm),
            chw=hy_conv_w[l].T, chb=hy_conv_b[l].reshape(3 * dh, 1),
            bg=jnp.pad(b_gates[l], (0, GATE_PAD - ngate)).reshape(1, GATE_PAD),
            mh_gain=mh_gain[l].reshape(1, dm),
            fw1=filt_w1[l], fb1=filt_b1[l], ffreq=filt_freq[l], fw2=filt_w2[l], fb2=filt_b2[l], fw3=filt_w3[l],
            fbias=filt_bias[l], wa=w_out[l, :dm].astype(BF16), wb=w_out[l, dm:].astype(BF16),
        )
        ml = mod[l]
        split3 = lambda a: (a[..., :d], a[..., d:2 * d], a[..., 2 * d:])
        shift_l, scale_l, gate_l = split3(ml[:bsz].reshape(bsz, 1, 3 * d))
        shift_c, scale_c, gate_c = split3(jnp.broadcast_to(ml[bsz].reshape(1, 1, 3 * d), (bsz, 1, 3 * d)))
        new_xc, ctx_state = _mixer(xc, scale_c, shift_c, gate_c, zero_state, lw, group=xc.shape[1], need_out=not last)
        xl, _ = _mixer(xl, scale_l, shift_l, gate_l, ctx_state, lw, group=GRID_W, need_out=True)
        if not last:
            xc = new_xc
    return xl
```

```python
import functools
import math

import jax
import jax.numpy as jnp
from jax import lax
from jax.experimental import pallas as pl
from jax.experimental.pallas import tpu as pltpu

F32 = jnp.float32
BF16 = jnp.bfloat16
HI = lax.Precision.HIGHEST

LANES = 128
SUBLANES = 8
BF16_ROWS = 2 * SUBLANES
MXU_DIM = 256
VMEM_LIMIT = 56 * 1024 * 1024

GRID_W = 64
MLSTM_HEADS = 4
HYENA_ORDER = 2
FILTER_BANDS = 8
DECAY_TARGET = 1e-2
FAST_DECAY_PCT = 0.3
SLOW_DECAY_PCT = 1.5
MIN_DECAY = math.log(DECAY_TARGET) / SLOW_DECAY_PCT
MAX_DECAY = math.log(DECAY_TARGET) / FAST_DECAY_PCT
RMS_EPS = 1e-6

TBLK = MXU_DIM
CONV_CB = SUBLANES
CONV_STEP = 16
CONV_DEPTH = 3
INPROJ_ROWS = 2 * TBLK
OUTPROJ_ROWS = 2 * TBLK
GATE_PAD = LANES
MLSTM_CHUNK = LANES
MLSTM_BLOCK = 4 * MLSTM_CHUNK
LOG2E = math.log2(math.e)
NT_DIMS = (((1,), (1,)), ((), ()))


def _params(*sem):
    return pltpu.CompilerParams(dimension_semantics=sem, vmem_limit_bytes=VMEM_LIMIT)


def _silu(x):
    return x * jax.nn.sigmoid(x)


def _mod_kernel(c_ref, w_ref, b_ref, o_ref):
    s = _silu(c_ref[...])
    o_ref[0] = jnp.dot(s, w_ref[0], precision=HI, preferred_element_type=F32) + b_ref[0]


def _modulation(cc, w_mod, b_mod):
    depth, d, d3 = w_mod.shape
    rows = cc.shape[0]
    nb = d3 // d
    return pl.pallas_call(
        _mod_kernel,
        grid=(depth, nb),
        in_specs=[
            pl.BlockSpec((rows, d), lambda l, n: (0, 0)),
            pl.BlockSpec((1, d, d), lambda l, n: (l, 0, n)),
            pl.BlockSpec((1, 1, d), lambda l, n: (l, 0, n)),
        ],
        out_specs=pl.BlockSpec((1, rows, d), lambda l, n: (l, 0, n)),
        out_shape=jax.ShapeDtypeStruct((depth, rows, d3), F32),
        compiler_params=_params("arbitrary", "arbitrary"),
        name="modulation",
    )(cc, w_mod, b_mod.reshape(depth, 1, d3))


def _inproj_kernel(x_ref, sc_ref, sh_ref, pg_ref, wn_ref, wt_ref, cqw_ref, cqb_ref, chw_ref, chb_ref, bg_ref,
                   mg_ref, q_ref, kt_ref, v_ref, og_ref, zg_ref, g_ref, hy_ref, *, group, dm, dh):
    tm = x_ref.shape[1]
    ts = min(tm, TBLK)
    subs = [slice(i * ts, (i + 1) * ts) for i in range(tm // ts)]
    hs = []
    for rs in subs:
        x = x_ref[0, rs, :]
        ms = jnp.mean(x * x, axis=-1, keepdims=True)
        y = x * lax.rsqrt(ms + RMS_EPS) * pg_ref[...]
        hs.append((y * (1.0 + sc_ref[0]) + sh_ref[0]).astype(BF16))
    hd = dm // MLSTM_HEADS
    proj = lambda h, lo, hi: jnp.dot(h, wn_ref[:, lo:hi], preferred_element_type=F32)

    tg = lax.broadcasted_iota(jnp.int32, (ts, 1), 0) % group
    glane = lax.broadcasted_iota(jnp.int32, (1, GATE_PAD), 1)
    for part in range(2):
        cs = slice(part * dm, (part + 1) * dm)
        for rs, h in zip(subs, hs):
            uqk = proj(h, part * dm, (part + 1) * dm)
            prev = jnp.where(tg == 0, 0.0, pltpu.roll(uqk, 1, axis=0))
            nxt = jnp.where(tg == group - 1, 0.0, pltpu.roll(uqk, ts - 1, axis=0))
            sq = _silu(prev * cqw_ref[0:1, cs] + uqk * cqw_ref[1:2, cs] + nxt * cqw_ref[2:3, cs] + cqb_ref[:, cs])
            if part == 0:
                q_ref[0, rs, :] = sq.astype(BF16)
            else:
                kt_ref[0, :, rs] = (sq * (hd ** -0.5)).T.astype(BF16)
    for rs, h in zip(subs, hs):
        v_ref[0, rs, :] = proj(h, 2 * dm, 3 * dm).astype(BF16)
    for rs, h in zip(subs, hs):
        og_ref[0, rs, :] = jax.nn.sigmoid(proj(h, 3 * dm, 4 * dm))
    for rs, h in zip(subs, hs):
        zg_ref[0, rs, :] = _silu(proj(h, 4 * dm, 5 * dm)) * mg_ref[...]
    for rs, h in zip(subs, hs):
        g2 = (proj(h, 5 * dm, 5 * dm + GATE_PAD) + bg_ref[...]) * LOG2E
        lf = jnp.minimum(g2, 0.0) - jnp.log2(1.0 + jnp.exp2(-jnp.abs(g2)))
        g_ref[0, rs, :] = jnp.where(glane % (2 * MLSTM_HEADS) >= MLSTM_HEADS, lf, g2)

    lg = lax.broadcasted_iota(jnp.int32, (1, ts), 1) % group
    ncb = dh // CONV_CB
    for part in range(4):
        ps = slice(part * dh, (part + 1) * dh)
        for tb, h in enumerate(hs):
            uh = lax.dot_general(wt_ref[ps, :], h, NT_DIMS, preferred_element_type=F32)
            if part < 3:
                prevh = jnp.where(lg == 0, 0.0, pltpu.roll(uh, 1, axis=1))
                nxth = jnp.where(lg == group - 1, 0.0, pltpu.roll(uh, ts - 1, axis=1))
                uh = prevh * chw_ref[ps, 0:1] + uh * chw_ref[ps, 1:2] + nxth * chw_ref[ps, 2:3] + chb_ref[ps, :]
            for half in range(ts // LANES):
                hy_ref[part * ncb:(part + 1) * ncb, half, tb, 0] = (
                    uh[:, half * LANES:(half + 1) * LANES].reshape(ncb, CONV_CB, LANES))


def _inproj(x, scale, shift, pre_gain, wn, wt, cqw, cqb, chw, chb, bg, mh_gain, *, group):
    bsz, seq, d = x.shape
    tm = min(seq, INPROJ_ROWS)
    nblk = seq // TBLK
    dm = (wn.shape[1] - GATE_PAD) // 5
    dh = wt.shape[0] // 4
    row = lambda b, t: (b, t, 0)
    const2 = lambda b, t: (0, 0)
    kern = functools.partial(_inproj_kernel, group=group, dm=dm, dh=dh)
    return pl.pallas_call(
        kern,
        grid=(bsz, seq // tm),
        in_specs=[
            pl.BlockSpec((1, tm, d), row),
            pl.BlockSpec((1, 1, d), lambda b, t: (b, 0, 0)),
            pl.BlockSpec((1, 1, d), lambda b, t: (b, 0, 0)),
            pl.BlockSpec((1, d), const2),
            pl.BlockSpec(wn.shape, const2, pipeline_mode=pl.Buffered(1)),
            pl.BlockSpec(wt.shape, const2, pipeline_mode=pl.Buffered(1)),
            pl.BlockSpec(cqw.shape, const2),
            pl.BlockSpec(cqb.shape, const2),
            pl.BlockSpec(chw.shape, const2),
            pl.BlockSpec(chb.shape, const2),
            pl.BlockSpec(bg.shape, const2),
            pl.BlockSpec(mh_gain.shape, const2),
        ],
        out_specs=[
            pl.BlockSpec((1, tm, dm), row),
            pl.BlockSpec((1, dm, tm), lambda b, t: (b, 0, t)),
            pl.BlockSpec((1, tm, dm), row),
            pl.BlockSpec((1, tm, dm), row),
            pl.BlockSpec((1, tm, dm), row),
            pl.BlockSpec((1, tm, GATE_PAD), row),
            pl.BlockSpec((4 * dh // CONV_CB, TBLK // LANES, tm // TBLK, 1, CONV_CB, LANES),
                         lambda b, t: (0, 0, t, b, 0, 0)),
        ],
        out_shape=[
            jax.ShapeDtypeStruct((bsz, seq, dm), BF16),
            jax.ShapeDtypeStruct((bsz, dm, seq), BF16),
            jax.ShapeDtypeStruct((bsz, seq, dm), BF16),
            jax.ShapeDtypeStruct((bsz, seq, dm), F32),
            jax.ShapeDtypeStruct((bsz, seq, dm), F32),
            jax.ShapeDtypeStruct((bsz, seq, GATE_PAD), F32),
            jax.ShapeDtypeStruct((4 * dh // CONV_CB, TBLK // LANES, nblk, bsz, CONV_CB, LANES), F32),
        ],
        compiler_params=_params("arbitrary", "arbitrary"),
        name="inproj",
    )(x, scale, shift, pre_gain, wn, wt, cqw, cqb, chw, chb, bg, mh_gain)


def _mlstm_kernel(q_ref, kt_ref, v_ref, g_ref, og_ref, zg_ref, c0_ref, m0_ref,
                  out_ref, cf_ref, mf_ref, hb_ref, c_ref, m_ref, *, tb, lc, nc, nh, dh):
    phase = pl.program_id(1)
    j = pl.program_id(2)

    @pl.when((phase == 0) & (j == 0))
    def _():
        m_ref[...] = m0_ref[0]

    def run(rev):
        d = 1 if rev else 0

        @pl.when(j == 0)
        def _():
            c_ref[...] = c0_ref[0, d]

        row0 = pl.multiple_of(((nc - 1 - j) if rev else j) * tb, tb)
        r = lax.broadcasted_iota(jnp.int32, (lc, lc), 0)
        s = lax.broadcasted_iota(jnp.int32, (lc, lc), 1)
        causal = (s >= r) if rev else (s <= r)
        tri = causal.astype(BF16)
        ones = jnp.ones((lc, dh), BF16)
        lanes = lax.broadcasted_iota(jnp.int32, (1, GATE_PAD), 1)
        mine = (lanes >= 2 * nh * d + nh) & (lanes < 2 * nh * (d + 1))
        subs = list(range(tb // lc))
        order = subs[::-1] if rev else subs
        pick1 = lambda row, fc: jnp.sum(jnp.where(lanes == fc, row, 0.0), axis=1, keepdims=True)

        prep = {}
        mrow = m_ref[...]
        for sc in order:
            rs = slice(sc * lc, (sc + 1) * lc)
            g = g_ref[0, rs, :]
            lf = g
            hi = lf.astype(BF16)
            rem = lf - hi.astype(F32)
            mid = rem.astype(BF16)
            lo = (rem - mid.astype(F32)).astype(BF16)
            bcum = (jnp.dot(tri, hi, preferred_element_type=F32) + jnp.dot(tri, mid, preferred_element_type=F32)
                    + jnp.dot(tri, lo, preferred_element_type=F32))
            a = pltpu.roll(g, nh, axis=1) - bcum
            amt = jnp.maximum(jnp.max(a, axis=0, keepdims=True), mrow)
            prep[sc] = (bcum, a.T, mrow, amt, jnp.exp2(mrow - amt))
            mrow = jnp.where(mine, jnp.sum(lf, axis=0, keepdims=True) + amt, mrow)
        m_ref[...] = mrow

        wts = {}
        for sc in order:
            bcum, at, mrow_c, amt, wcrow = prep[sc]
            for h in range(nh):
                fc = 2 * nh * d + nh + h
                mprev = pick1(mrow_c, fc)
                arow = at[fc:fc + 1, :]
                am = jnp.where(causal, arow, -jnp.inf)
                mtb = jnp.broadcast_to(jnp.maximum(jnp.max(am, axis=1, keepdims=True), mprev), (lc, dh))
                bcol = jnp.broadcast_to(bcum[:, fc:fc + 1], (lc, dh))
                wts[sc, h] = (jnp.exp2(am - mtb), jnp.exp2(mprev - mtb), jnp.exp2(-(bcol + mtb)),
                              jnp.exp2(arow - pick1(amt, fc)), pick1(wcrow, fc))

        for sc in order:
            rs = slice(sc * lc, (sc + 1) * lc)
            rows = pl.ds(row0 + sc * lc, lc)
            for h in range(nh):
                hs = slice(h * dh, (h + 1) * dh)
                e, wq, emt, wsrow, wc = wts[sc, h]
                qh = q_ref[0, rs, hs]
                kth = kt_ref[0, hs, rs]
                vaug = jnp.concatenate([v_ref[0, rs, hs], ones], axis=1)
                caug = c_ref[h]
                sm = jnp.dot(qh, kth, preferred_element_type=F32) * e
                qw = qh.astype(F32) * wq
                lhs = jnp.concatenate([sm.astype(BF16), qw.astype(BF16)], axis=1)
                rhs = jnp.concatenate([vaug, caug.astype(BF16)], axis=0)
                nd = jnp.dot(lhs, rhs, preferred_element_type=F32)
                hh = nd[:, :dh] / jnp.maximum(jnp.abs(nd[:, dh:]), emt)
                kwt = (kth.astype(F32) * wsrow).astype(BF16)
                c_ref[h] = wc * caug + jnp.dot(kwt, vaug, preferred_element_type=F32)
                if rev:
                    hb_ref[rows, hs] = hh
                else:
                    ho = (hh + hb_ref[rows, hs]) * og_ref[0, rs, hs]
                    ho = ho * lax.rsqrt(jnp.mean(ho * ho, axis=-1, keepdims=True) + RMS_EPS)
                    out_ref[0, rs, hs] = ho * zg_ref[0, rs, hs]

        @pl.when(j == nc - 1)
        def _():
            cf_ref[0, d] = c_ref[...]
            if not rev:
                mf_ref[0] = m_ref[...]

    pl.when(phase == 0)(functools.partial(run, True))
    pl.when(phase == 1)(functools.partial(run, False))


def _mlstm(q, kt, v, g, og, zg, c0, m0):
    bsz, seq, dmw = q.shape
    nh = MLSTM_HEADS
    dh = dmw // nh
    tb = min(seq, MLSTM_BLOCK)
    nc = seq // tb
    chunk = lambda b, p, j: (b, jnp.where(p == 0, nc - 1 - j, j), 0)
    chunk_t = lambda b, p, j: (b, 0, jnp.where(p == 0, nc - 1 - j, j))
    fwd_only = lambda b, p, j: (b, j * p, 0)
    st5 = lambda b, p, j: (b, 0, 0, 0, 0)
    st3 = lambda b, p, j: (b, 0, 0)
    kern = functools.partial(_mlstm_kernel, tb=tb, lc=MLSTM_CHUNK, nc=nc, nh=nh, dh=dh)
    return pl.pallas_call(
        kern,
        grid=(bsz, 2, nc),
        in_specs=[
            pl.BlockSpec((1, tb, dmw), chunk),
            pl.BlockSpec((1, dmw, tb), chunk_t),
            pl.BlockSpec((1, tb, dmw), chunk),
            pl.BlockSpec((1, tb, GATE_PAD), chunk),
            pl.BlockSpec((1, tb, dmw), fwd_only),
            pl.BlockSpec((1, tb, dmw), fwd_only),
            pl.BlockSpec((1, 2, nh, dh, 2 * dh), st5),
            pl.BlockSpec((1, 1, GATE_PAD), st3),
        ],
        out_specs=[
            pl.BlockSpec((1, tb, dmw), fwd_only),
            pl.BlockSpec((1, 2, nh, dh, 2 * dh), st5),
            pl.BlockSpec((1, 1, GATE_PAD), st3),
        ],
        out_shape=[
            jax.ShapeDtypeStruct((bsz, seq, dmw), F32),
            jax.ShapeDtypeStruct((bsz, 2, nh, dh, 2 * dh), F32),
            jax.ShapeDtypeStruct((bsz, 1, GATE_PAD), F32),
        ],
        scratch_shapes=[
            pltpu.VMEM((seq, dmw), F32),
            pltpu.VMEM((nh, dh, 2 * dh), F32),
            pltpu.VMEM((1, GATE_PAD), F32),
        ],
        compiler_params=_params("arbitrary", "arbitrary", "arbitrary"),
        name="mlstm",
    )(q, kt, v, g, og, zg, c0, m0)


def _conv_geometry(seq):
    nblk = seq // TBLK
    dmax = min(nblk - 1, (seq // 2 + TBLK - 2) // TBLK)
    ncols = (2 * dmax + 1) * TBLK
    n_base = seq // 2 + TBLK * (1 - dmax)
    lpad = n_base + ncols
    return nblk, dmax, ncols, n_base, lpad


def _filter_kernel(w1_ref, b1_ref, fr_ref, w2_ref, b2_ref, w3_ref, rate_ref, o_ref, h2_ref, *, seq, lpad):
    pos = lax.broadcasted_iota(jnp.int32, (1, seq), 1).astype(F32)

    @pl.when(pl.program_id(0) == 0)
    def _():
        bands = (lax.broadcasted_iota(jnp.int32, (FILTER_BANDS, 1), 0) + 1).astype(F32)
        ang = (2.0 * math.pi / seq) * pos * bands
        sn = jnp.sin(ang)
        cs = jnp.cos(ang)
        pre = b1_ref[...] + w1_ref[:, 0:1] * (pos / seq)
        for kb in range(FILTER_BANDS):
            pre = pre + w1_ref[:, 1 + kb:2 + kb] * sn[kb:kb + 1, :]
            pre = pre + w1_ref[:, 1 + FILTER_BANDS + kb:2 + FILTER_BANDS + kb] * cs[kb:kb + 1, :]
        h1 = jnp.sin(fr_ref[:, 0:1] * pre)
        h2_ref[...] = jnp.sin(fr_ref[:, 1:2] * (
            jnp.dot(w2_ref[...], h1, precision=HI, preferred_element_type=F32) + b2_ref[...]))

    f = jnp.dot(w3_ref[...], h2_ref[...], precision=HI, preferred_element_type=F32)
    offset = jnp.abs(pos - (seq // 2)) / (seq / 2)
    f = f * jnp.exp(-offset * rate_ref[...])
    f = f * lax.rsqrt(jnp.sum(f * f, axis=1, keepdims=True) + RMS_EPS)
    rows = f.shape[0]
    first = TBLK // LANES
    for jj in range(lpad // LANES):
        if first <= jj < first + seq // LANES:
            o_ref[jj] = f[:, (jj - first) * LANES:(jj - first + 1) * LANES]
        else:
            o_ref[jj] = jnp.zeros((rows, LANES), F32)


def _filters(seq, w1, b1, freq, w2, b2, w3):
    hidden = w2.shape[0]
    nrow = w3.shape[1]
    c = nrow // HYENA_ORDER
    _, _, _, _, lpad = _conv_geometry(seq)
    rates = jnp.abs(jnp.linspace(MIN_DECAY, MAX_DECAY, c, dtype=F32))
    rates = jnp.tile(rates, HYENA_ORDER).reshape(nrow, 1)
    rb = LANES
    const = lambda i: (0, 0)
    out = pl.pallas_call(
        functools.partial(_filter_kernel, seq=seq, lpad=lpad),
        grid=(nrow // rb,),
        in_specs=[
            pl.BlockSpec((hidden, w1.shape[0]), const),
            pl.BlockSpec((hidden, 1), const),
            pl.BlockSpec((hidden, 2), const),
            pl.BlockSpec((hidden, hidden), const),
            pl.BlockSpec((hidden, 1), const),
            pl.BlockSpec((rb, hidden), lambda i: (i, 0)),
            pl.BlockSpec((rb, 1), lambda i: (i, 0)),
        ],
        out_specs=pl.BlockSpec((lpad // LANES, rb, LANES), lambda i: (0, i, 0)),
        out_shape=jax.ShapeDtypeStruct((lpad // LANES, nrow, LANES), F32),
        scratch_shapes=[pltpu.VMEM((hidden, seq), F32)],
        compiler_params=_params("arbitrary"),
        name="hyena_filter",
    )(w1.T, b1.reshape(hidden, 1), freq.T, w2.T, b2.reshape(hidden, 1), w3.T, rates)
    return out


def _longconv_kernel(*refs, m, mp, pad, bsz, cb, nch, dmax, nj, jt0, nb, with_zh):
    if with_zh:
        fb_ref, u_ref, x_ref, zh_ref, f_ref, o_ref, p_ref, a_ref, uf_ref, acc_ref = refs
    else:
        fb_ref, u_ref, x_ref, f_ref, o_ref, p_ref, a_ref, uf_ref, acc_ref = refs
        zh_ref = None
    mc = m * cb
    base = pl.program_id(0) * nch
    p_ref[...] = jnp.zeros(p_ref.shape, BF16)
    lane = lax.broadcasted_iota(jnp.int32, (LANES, LANES), 1)
    sub = lax.broadcasted_iota(jnp.int32, (LANES, LANES), 0)
    keep = lane >= sub

    def rows(ref, c):
        return jnp.concatenate([ref[c // cb, pl.ds(hf * mc + c % cb, m, stride=cb), :]
                                for hf in range(TBLK // LANES)], axis=1)

    def stage(c, slot):
        u = rows(u_ref, c)
        uf_ref[slot] = u
        p_ref[slot, 0, pad:pad + m, :] = u.astype(BF16)
        p_ref[slot, 1, pad:pad + bsz + m, :] = jnp.concatenate([jnp.zeros((bsz, TBLK), F32), u], axis=0).astype(BF16)
        prev = None
        for jj in range(jt0 - 2, jt0 + nb):
            tap = jnp.broadcast_to(f_ref[jj, c:c + 1, :], (LANES, LANES))
            cur = pltpu.roll(tap, 0, axis=1, stride=1, stride_axis=0).astype(BF16)
            if prev is not None:
                blk = jnp.where(keep, cur, prev)
                if jj >= jt0:
                    a_ref[slot, :LANES, (jj - jt0) * LANES:(jj - jt0 + 1) * LANES] = blk
                if jj + 1 < jt0 + nb:
                    a_ref[slot, LANES:, (jj + 1 - jt0) * LANES:(jj + 2 - jt0) * LANES] = blk
            prev = cur

    def convolve(slot):
        acc = jnp.zeros((mp, TBLK), F32)
        for d in range(-dmax, dmax + 1):
            start = pad - bsz * (d - d % 2)
            lhs = p_ref[slot, d % 2, start:start + mp, :]
            col = (d + dmax) * TBLK
            acc = acc + jnp.dot(lhs, a_ref[slot, :, col:col + TBLK], preferred_element_type=F32)
        acc_ref[slot] = acc[:m]

    def finish(c, slot):
        y = rows(x_ref, c) * (acc_ref[slot] + uf_ref[slot] * fb_ref[base + c])
        if with_zh:
            y = y * _silu(rows(zh_ref, c))
        for hf in range(TBLK // LANES):
            o_ref[c // cb, pl.ds(hf * mc + c % cb, m, stride=cb), :] = y[:, hf * LANES:(hf + 1) * LANES]

    stage(0, 0)
    stage(1, 1)
    convolve(0)
    for c in range(1, nch):
        finish(c - 1, (c - 1) % CONV_DEPTH)
        if c + 1 < nch:
            stage(c + 1, (c + 1) % CONV_DEPTH)
        convolve(c % CONV_DEPTH)
    finish(nch - 1, (nch - 1) % CONV_DEPTH)


def _longconv(u, u_off, x, x_off, zh, zh_off, filt, f_off, fbias, seq):
    _, nhalf, nblk, bsz, cb, _ = u.shape
    c = fbias.shape[0]
    _, dmax, ncols, n_base, lpad = _conv_geometry(seq)
    nj = lpad // LANES
    m = nblk * bsz
    mp = max(m, 2 * SUBLANES)
    pad = -(-bsz * dmax // BF16_ROWS) * BF16_ROWS
    with_zh = zh is not None
    rows = nhalf * m * cb
    flat = lambda a: a.reshape(a.shape[0], rows, LANES)
    nch = CONV_STEP
    kb = nch // cb
    spec = lambda off: pl.BlockSpec((kb, rows, LANES), lambda i: (off // nch + i, 0, 0))
    in_specs = [pl.BlockSpec(memory_space=pltpu.SMEM), spec(u_off), spec(x_off)]
    args = [fbias, flat(u), flat(x)]
    if with_zh:
        in_specs.append(spec(zh_off))
        args.append(flat(zh))
    in_specs.append(pl.BlockSpec((nj, nch, LANES), lambda i: (0, f_off // nch + i, 0)))
    args.append(filt)
    kern = functools.partial(_longconv_kernel, m=m, mp=mp, pad=pad, bsz=bsz, cb=cb, nch=nch, dmax=dmax, nj=nj,
                             jt0=n_base // LANES, nb=ncols // LANES, with_zh=with_zh)
    out = pl.pallas_call(
        kern,
        grid=(c // nch,),
        in_specs=in_specs,
        out_specs=pl.BlockSpec((kb, rows, LANES), lambda i: (i, 0, 0)),
        out_shape=jax.ShapeDtypeStruct((c // cb, rows, LANES), F32),
        scratch_shapes=[
            pltpu.VMEM((CONV_DEPTH, 2, 2 * pad + bsz + mp, TBLK), BF16),
            pltpu.VMEM((CONV_DEPTH, TBLK, ncols), BF16),
            pltpu.VMEM((CONV_DEPTH, m, TBLK), F32),
            pltpu.VMEM((CONV_DEPTH, m, TBLK), F32),
        ],
        compiler_params=_params("arbitrary"),
        name="hyena_longconv",
    )(*args)
    return out.reshape(c // cb, nhalf, nblk, bsz, cb, LANES)


def _outproj_kernel(m_ref, yh_ref, x_ref, wa_ref, wb_ref, pg_ref, gate_ref, o_ref):
    ncb, nhalf, ntb, _, cb, _ = yh_ref.shape
    for tb in range(ntb):
        rs = slice(tb * TBLK, (tb + 1) * TBLK)
        mm = m_ref[0, rs, :].astype(BF16)
        yh = jnp.concatenate([yh_ref[:, hf, tb, 0].reshape(ncb * cb, LANES) for hf in range(nhalf)], axis=1)
        y = (jnp.dot(mm, wa_ref[...], preferred_element_type=F32)
             + jnp.dot(yh.T.astype(BF16), wb_ref[...], preferred_element_type=F32))
        r = y * lax.rsqrt(jnp.mean(y * y, axis=-1, keepdims=True) + RMS_EPS) * pg_ref[...]
        o_ref[0, rs, :] = x_ref[0, rs, :] + gate_ref[0] * r


def _outproj(mm, yh, x, wa, wb, post_gain, gate):
    bsz, seq, d = x.shape
    tm = min(seq, OUTPROJ_ROWS)
    dm = mm.shape[2]
    ncb, nhalf, _, _, cb, _ = yh.shape
    row = lambda b, t: (b, t, 0)
    const2 = lambda b, t: (0, 0)
    return pl.pallas_call(
        _outproj_kernel,
        grid=(bsz, seq // tm),
        in_specs=[
            pl.BlockSpec((1, tm, dm), row),
            pl.BlockSpec((ncb, nhalf, tm // TBLK, 1, cb, LANES), lambda b, t: (0, 0, t, b, 0, 0)),
            pl.BlockSpec((1, tm, d), row),
            pl.BlockSpec(wa.shape, const2, pipeline_mode=pl.Buffered(1)),
            pl.BlockSpec(wb.shape, const2, pipeline_mode=pl.Buffered(1)),
            pl.BlockSpec((1, d), const2),
            pl.BlockSpec((1, 1, d), lambda b, t: (b, 0, 0)),
        ],
        out_specs=pl.BlockSpec((1, tm, d), row),
        out_shape=jax.ShapeDtypeStruct((bsz, seq, d), F32),
        compiler_params=_params("arbitrary", "arbitrary"),
        name="outproj",
    )(mm, yh, x, wa, wb, post_gain, gate)


def _mixer(x, scale, shift, gate, state, lw, *, group, need_out):
    seq = x.shape[1]
    q, kt, v, og, zg, g, hy = _inproj(x, scale, shift, lw["pre_gain"], lw["wn"], lw["wt"], lw["cqw"], lw["cqb"],
                                     lw["chw"], lw["chb"], lw["bg"], lw["mh_gain"], group=group)
    mm, cf, mf = _mlstm(q, kt, v, g, og, zg, *state)
    if not need_out:
        return None, (cf, mf)
    dh = lw["fbias"].shape[1]
    filt = _filters(seq, lw["fw1"], lw["fb1"], lw["ffreq"], lw["fw2"], lw["fb2"], lw["fw3"])
    z = _longconv(hy, 0, hy, dh, None, 0, filt, 0, lw["fbias"][0], seq)
    yh = _longconv(z, 0, hy, 2 * dh, hy, 3 * dh, filt, dh, lw["fbias"][1], seq)
    new_x = _outproj(mm, yh, x, lw["wa"], lw["wb"], lw["post_gain"], gate)
    return new_x, (cf, mf)


def kernel(x, c, ctx, c_ctx, w_mod, b_mod, pre_gain, post_gain, w_in, b_gates, conv_qk_w, conv_qk_b, mh_gain,
           hy_conv_w, hy_conv_b, filt_w1, filt_b1, filt_freq, filt_w2, filt_b2, filt_w3, filt_bias, w_out):
    bsz, _, d = x.shape
    depth = w_mod.shape[0]
    dm = mh_gain.shape[1]
    dh = filt_bias.shape[2]
    nh = MLSTM_HEADS
    hd = dm // nh
    ngate = b_gates.shape[1]
    o_gate = 5 * dm
    o_hy = o_gate + ngate

    mod_rows = 2 * SUBLANES
    cc = jnp.concatenate([c, c_ctx[None, :], jnp.zeros((mod_rows - bsz - 1, d), F32)], axis=0)
    mod = _modulation(cc, w_mod, b_mod)

    zero_state = (jnp.zeros((bsz, 2, nh, hd, 2 * hd), F32), jnp.zeros((bsz, 1, GATE_PAD), F32))
    xl, xc = x, ctx
    for l in range(depth):
        last = l == depth - 1
        w_l = w_in[l].astype(BF16)
        gate_w = jnp.pad(w_l[:, o_gate:o_hy], ((0, 0), (0, GATE_PAD - ngate)))
        lw = dict(
            pre_gain=pre_gain[l].reshape(1, d), post_gain=post_gain[l].reshape(1, d),
            wn=jnp.concatenate([w_l[:, :o_gate], gate_w], axis=1), wt=w_l[:, o_hy:].T,
            cqw=conv_qk_w[l], cqb=conv_qk_b[l].reshape(1, 2 * dm),
            chw=hy_conv_w[l].T, chb=hy_conv_b[l].reshape(3 * dh, 1),
            bg=jnp.pad(b_gates[l], (0, GATE_PAD - ngate)).reshape(1, GATE_PAD),
            mh_gain=mh_gain[l].reshape(1, dm),
            fw1=filt_w1[l], fb1=filt_b1[l], ffreq=filt_freq[l], fw2=filt_w2[l], fb2=filt_b2[l], fw3=filt_w3[l],
            fbias=filt_bias[l], wa=w_out[l, :dm].astype(BF16), wb=w_out[l, dm:].astype(BF16),
        )
        ml = mod[l]
        split3 = lambda a: (a[..., :d], a[..., d:2 * d], a[..., 2 * d:])
        shift_l, scale_l, gate_l = split3(ml[:bsz].reshape(bsz, 1, 3 * d))
        shift_c, scale_c, gate_c = split3(jnp.broadcast_to(ml[bsz].reshape(1, 1, 3 * d), (bsz, 1, 3 * d)))
        new_xc, ctx_state = _mixer(xc, scale_c, shift_c, gate_c, zero_state, lw, group=xc.shape[1], need_out=not last)
        xl, _ = _mixer(xl, scale_l, shift_l, gate_l, ctx_state, lw, group=GRID_W, need_out=True)
        if not last:
            xc = new_xc
    return xl
```

```python
import functools
import math

import jax
import jax.numpy as jnp
from jax import lax
from jax.experimental import pallas as pl
from jax.experimental.pallas import tpu as pltpu

F32 = jnp.float32
BF16 = jnp.bfloat16
HI = lax.Precision.HIGHEST

LANES = 128
SUBLANES = 8
BF16_ROWS = 2 * SUBLANES
MXU_DIM = 256
VMEM_LIMIT = 56 * 1024 * 1024

GRID_W = 64
MLSTM_HEADS = 4
HYENA_ORDER = 2
FILTER_BANDS = 8
DECAY_TARGET = 1e-2
FAST_DECAY_PCT = 0.3
SLOW_DECAY_PCT = 1.5
MIN_DECAY = math.log(DECAY_TARGET) / SLOW_DECAY_PCT
MAX_DECAY = math.log(DECAY_TARGET) / FAST_DECAY_PCT
RMS_EPS = 1e-6

TBLK = MXU_DIM
CONV_CB = SUBLANES
CONV_STEP = 16
CONV_DEPTH = 3
INPROJ_ROWS = 2 * TBLK
OUTPROJ_ROWS = 2 * TBLK
GATE_PAD = LANES
MLSTM_CHUNK = LANES
MLSTM_BLOCK = 8 * MLSTM_CHUNK
LOG2E = math.log2(math.e)
NT_DIMS = (((1,), (1,)), ((), ()))


def _params(*sem):
    return pltpu.CompilerParams(dimension_semantics=sem, vmem_limit_bytes=VMEM_LIMIT)


def _silu(x):
    return x * jax.nn.sigmoid(x)


def _mod_kernel(c_ref, w_ref, b_ref, o_ref):
    s = _silu(c_ref[...])
    o_ref[0] = jnp.dot(s, w_ref[0], precision=HI, preferred_element_type=F32) + b_ref[0]


def _modulation(cc, w_mod, b_mod):
    depth, d, d3 = w_mod.shape
    rows = cc.shape[0]
    nb = d3 // d
    return pl.pallas_call(
        _mod_kernel,
        grid=(depth, nb),
        in_specs=[
            pl.BlockSpec((rows, d), lambda l, n: (0, 0)),
            pl.BlockSpec((1, d, d), lambda l, n: (l, 0, n)),
            pl.BlockSpec((1, 1, d), lambda l, n: (l, 0, n)),
        ],
        out_specs=pl.BlockSpec((1, rows, d), lambda l, n: (l, 0, n)),
        out_shape=jax.ShapeDtypeStruct((depth, rows, d3), F32),
        compiler_params=_params("arbitrary", "arbitrary"),
        name="modulation",
    )(cc, w_mod, b_mod.reshape(depth, 1, d3))


def _inproj_kernel(x_ref, sc_ref, sh_ref, pg_ref, wn_ref, wt_ref, cqw_ref, cqb_ref, chw_ref, chb_ref, bg_ref,
                   mg_ref, q_ref, kt_ref, v_ref, og_ref, zg_ref, g_ref, hy_ref, *, group, dm, dh):
    tm = x_ref.shape[1]
    ts = min(tm, TBLK)
    subs = [slice(i * ts, (i + 1) * ts) for i in range(tm // ts)]
    hs = []
    for rs in subs:
        x = x_ref[0, rs, :]
        ms = jnp.mean(x * x, axis=-1, keepdims=True)
        y = x * lax.rsqrt(ms + RMS_EPS) * pg_ref[...]
        hs.append((y * (1.0 + sc_ref[0]) + sh_ref[0]).astype(BF16))
    hd = dm // MLSTM_HEADS
    proj = lambda h, lo, hi: jnp.dot(h, wn_ref[:, lo:hi], preferred_element_type=F32)

    tg = lax.broadcasted_iota(jnp.int32, (ts, 1), 0) % group
    glane = lax.broadcasted_iota(jnp.int32, (1, GATE_PAD), 1)
    for part in range(2):
        cs = slice(part * dm, (part + 1) * dm)
        for rs, h in zip(subs, hs):
            uqk = proj(h, part * dm, (part + 1) * dm)
            prev = jnp.where(tg == 0, 0.0, pltpu.roll(uqk, 1, axis=0))
            nxt = jnp.where(tg == group - 1, 0.0, pltpu.roll(uqk, ts - 1, axis=0))
            sq = _silu(prev * cqw_ref[0:1, cs] + uqk * cqw_ref[1:2, cs] + nxt * cqw_ref[2:3, cs] + cqb_ref[:, cs])
            if part == 0:
                q_ref[0, rs, :] = sq.astype(BF16)
            else:
                kt_ref[0, :, rs] = (sq * (hd ** -0.5)).T.astype(BF16)
    for rs, h in zip(subs, hs):
        v_ref[0, rs, :] = proj(h, 2 * dm, 3 * dm).astype(BF16)
    for rs, h in zip(subs, hs):
        og_ref[0, rs, :] = jax.nn.sigmoid(proj(h, 3 * dm, 4 * dm))
    for rs, h in zip(subs, hs):
        zg_ref[0, rs, :] = _silu(proj(h, 4 * dm, 5 * dm)) * mg_ref[...]
    for rs, h in zip(subs, hs):
        g2 = (proj(h, 5 * dm, 5 * dm + GATE_PAD) + bg_ref[...]) * LOG2E
        lf = jnp.minimum(g2, 0.0) - jnp.log2(1.0 + jnp.exp2(-jnp.abs(g2)))
        g_ref[0, rs, :] = jnp.where(glane % (2 * MLSTM_HEADS) >= MLSTM_HEADS, lf, g2)

    lg = lax.broadcasted_iota(jnp.int32, (1, ts), 1) % group
    ncb = dh // CONV_CB
    for part in range(4):
        ps = slice(part * dh, (part + 1) * dh)
        for tb, h in enumerate(hs):
            uh = lax.dot_general(wt_ref[ps, :], h, NT_DIMS, preferred_element_type=F32)
            if part < 3:
                prevh = jnp.where(lg == 0, 0.0, pltpu.roll(uh, 1, axis=1))
                nxth = jnp.where(lg == group - 1, 0.0, pltpu.roll(uh, ts - 1, axis=1))
                uh = prevh * chw_ref[ps, 0:1] + uh * chw_ref[ps, 1:2] + nxth * chw_ref[ps, 2:3] + chb_ref[ps, :]
            for half in range(ts // LANES):
                hy_ref[part * ncb:(part + 1) * ncb, half, tb, 0] = (
                    uh[:, half * LANES:(half + 1) * LANES].reshape(ncb, CONV_CB, LANES))


def _inproj(x, scale, shift, pre_gain, wn, wt, cqw, cqb, chw, chb, bg, mh_gain, *, group):
    bsz, seq, d = x.shape
    tm = min(seq, INPROJ_ROWS)
    nblk = seq // TBLK
    dm = (wn.shape[1] - GATE_PAD) // 5
    dh = wt.shape[0] // 4
    row = lambda b, t: (b, t, 0)
    const2 = lambda b, t: (0, 0)
    kern = functools.partial(_inproj_kernel, group=group, dm=dm, dh=dh)
    return pl.pallas_call(
        kern,
        grid=(bsz, seq // tm),
        in_specs=[
            pl.BlockSpec((1, tm, d), row),
            pl.BlockSpec((1, 1, d), lambda b, t: (b, 0, 0)),
            pl.BlockSpec((1, 1, d), lambda b, t: (b, 0, 0)),
            pl.BlockSpec((1, d), const2),
            pl.BlockSpec(wn.shape, const2, pipeline_mode=pl.Buffered(1)),
            pl.BlockSpec(wt.shape, const2, pipeline_mode=pl.Buffered(1)),
            pl.BlockSpec(cqw.shape, const2),
            pl.BlockSpec(cqb.shape, const2),
            pl.BlockSpec(chw.shape, const2),
            pl.BlockSpec(chb.shape, const2),
            pl.BlockSpec(bg.shape, const2),
            pl.BlockSpec(mh_gain.shape, const2),
        ],
        out_specs=[
            pl.BlockSpec((1, tm, dm), row),
            pl.BlockSpec((1, dm, tm), lambda b, t: (b, 0, t)),
            pl.BlockSpec((1, tm, dm), row),
            pl.BlockSpec((1, tm, dm), row),
            pl.BlockSpec((1, tm, dm), row),
            pl.BlockSpec((1, tm, GATE_PAD), row),
            pl.BlockSpec((4 * dh // CONV_CB, TBLK // LANES, tm // TBLK, 1, CONV_CB, LANES),
                         lambda b, t: (0, 0, t, b, 0, 0)),
        ],
        out_shape=[
            jax.ShapeDtypeStruct((bsz, seq, dm), BF16),
            jax.ShapeDtypeStruct((bsz, dm, seq), BF16),
            jax.ShapeDtypeStruct((bsz, seq, dm), BF16),
            jax.ShapeDtypeStruct((bsz, seq, dm), F32),
            jax.ShapeDtypeStruct((bsz, seq, dm), F32),
            jax.ShapeDtypeStruct((bsz, seq, GATE_PAD), F32),
            jax.ShapeDtypeStruct((4 * dh // CONV_CB, TBLK // LANES, nblk, bsz, CONV_CB, LANES), F32),
        ],
        compiler_params=_params("arbitrary", "arbitrary"),
        name="inproj",
    )(x, scale, shift, pre_gain, wn, wt, cqw, cqb, chw, chb, bg, mh_gain)


def _mlstm_kernel(q_ref, kt_ref, v_ref, g_ref, og_ref, zg_ref, c0_ref, m0_ref,
                  out_ref, cf_ref, mf_ref, hb_ref, c_ref, m_ref, *, tb, lc, nc, nh, dh):
    phase = pl.program_id(1)
    j = pl.program_id(2)

    @pl.when((phase == 0) & (j == 0))
    def _():
        m_ref[...] = m0_ref[0]

    def run(rev):
        d = 1 if rev else 0

        @pl.when(j == 0)
        def _():
            c_ref[...] = c0_ref[0, d]

        row0 = pl.multiple_of(((nc - 1 - j) if rev else j) * tb, tb)
        r = lax.broadcasted_iota(jnp.int32, (lc, lc), 0)
        s = lax.broadcasted_iota(jnp.int32, (lc, lc), 1)
        causal = (s >= r) if rev else (s <= r)
        tri = causal.astype(BF16)
        ones = jnp.ones((lc, dh), BF16)
        lanes = lax.broadcasted_iota(jnp.int32, (1, GATE_PAD), 1)
        mine = (lanes >= 2 * nh * d + nh) & (lanes < 2 * nh * (d + 1))
        subs = list(range(tb // lc))
        order = subs[::-1] if rev else subs
        pick1 = lambda row, fc: jnp.sum(jnp.where(lanes == fc, row, 0.0), axis=1, keepdims=True)

        prep = {}
        mrow = m_ref[...]
        for sc in order:
            rs = slice(sc * lc, (sc + 1) * lc)
            g = g_ref[0, rs, :]
            lf = g
            hi = lf.astype(BF16)
            rem = lf - hi.astype(F32)
            mid = rem.astype(BF16)
            lo = (rem - mid.astype(F32)).astype(BF16)
            bcum = (jnp.dot(tri, hi, preferred_element_type=F32) + jnp.dot(tri, mid, preferred_element_type=F32)
                    + jnp.dot(tri, lo, preferred_element_type=F32))
            a = pltpu.roll(g, nh, axis=1) - bcum
            amt = jnp.maximum(jnp.max(a, axis=0, keepdims=True), mrow)
            prep[sc] = (bcum, a.T, mrow, amt, jnp.exp2(mrow - amt))
            mrow = jnp.where(mine, jnp.sum(lf, axis=0, keepdims=True) + amt, mrow)
        m_ref[...] = mrow

        small = {}
        for sc in order:
            bcum, at, mrow_c, amt, wcrow = prep[sc]
            for h in range(nh):
                fc = 2 * nh * d + nh + h
                arow = at[fc:fc + 1, :]
                small[sc, h] = (arow, pick1(mrow_c, fc), jnp.exp2(arow - pick1(amt, fc)), pick1(wcrow, fc))

        mats = {}
        for sc in order:
            rs = slice(sc * lc, (sc + 1) * lc)
            bcum = prep[sc][0]
            for h in range(nh):
                fc = 2 * nh * d + nh + h
                hs = slice(h * dh, (h + 1) * dh)
                arow, mprev, _, _ = small[sc, h]
                am = jnp.where(causal, arow, -jnp.inf)
                mtb = jnp.broadcast_to(jnp.maximum(jnp.max(am, axis=1, keepdims=True), mprev), (lc, dh))
                bcol = jnp.broadcast_to(bcum[:, fc:fc + 1], (lc, dh))
                qh = q_ref[0, rs, hs]
                sm = jnp.dot(qh, kt_ref[0, hs, rs], preferred_element_type=F32) * jnp.exp2(am - mtb)
                qw = qh.astype(F32) * jnp.exp2(mprev - mtb)
                mats[sc, h] = (jnp.concatenate([sm.astype(BF16), qw.astype(BF16)], axis=1), jnp.exp2(-(bcol + mtb)))

        for sc in order:
            rs = slice(sc * lc, (sc + 1) * lc)
            rows = pl.ds(row0 + sc * lc, lc)
            for h in range(nh):
                hs = slice(h * dh, (h + 1) * dh)
                _, _, wsrow, wc = small[sc, h]
                lhs, emt = mats[sc, h]
                vaug = jnp.concatenate([v_ref[0, rs, hs], ones], axis=1)
                caug = c_ref[h]
                rhs = jnp.concatenate([vaug, caug.astype(BF16)], axis=0)
                nd = jnp.dot(lhs, rhs, preferred_element_type=F32)
                kwt = (kt_ref[0, hs, rs].astype(F32) * wsrow).astype(BF16)
                c_ref[h] = wc * caug + jnp.dot(kwt, vaug, preferred_element_type=F32)
                hh = nd[:, :dh] / jnp.maximum(jnp.abs(nd[:, dh:]), emt)
                if rev:
                    hb_ref[rows, hs] = hh
                else:
                    ho = (hh + hb_ref[rows, hs]) * og_ref[0, rs, hs]
                    ho = ho * lax.rsqrt(jnp.mean(ho * ho, axis=-1, keepdims=True) + RMS_EPS)
                    out_ref[0, rs, hs] = (ho * zg_ref[0, rs, hs]).astype(BF16)

        @pl.when(j == nc - 1)
        def _():
            cf_ref[0, d] = c_ref[...]
            if not rev:
                mf_ref[0] = m_ref[...]

    pl.when(phase == 0)(functools.partial(run, True))
    pl.when(phase == 1)(functools.partial(run, False))


def _mlstm(q, kt, v, g, og, zg, c0, m0):
    bsz, seq, dmw = q.shape
    nh = MLSTM_HEADS
    dh = dmw // nh
    tb = min(seq, MLSTM_BLOCK)
    nc = seq // tb
    chunk = lambda b, p, j: (b, jnp.where(p == 0, nc - 1 - j, j), 0)
    chunk_t = lambda b, p, j: (b, 0, jnp.where(p == 0, nc - 1 - j, j))
    fwd_only = lambda b, p, j: (b, j * p, 0)
    st5 = lambda b, p, j: (b, 0, 0, 0, 0)
    st3 = lambda b, p, j: (b, 0, 0)
    kern = functools.partial(_mlstm_kernel, tb=tb, lc=MLSTM_CHUNK, nc=nc, nh=nh, dh=dh)
    return pl.pallas_call(
        kern,
        grid=(bsz, 2, nc),
        in_specs=[
            pl.BlockSpec((1, tb, dmw), chunk),
            pl.BlockSpec((1, dmw, tb), chunk_t),
            pl.BlockSpec((1, tb, dmw), chunk),
            pl.BlockSpec((1, tb, GATE_PAD), chunk),
            pl.BlockSpec((1, tb, dmw), fwd_only),
            pl.BlockSpec((1, tb, dmw), fwd_only),
            pl.BlockSpec((1, 2, nh, dh, 2 * dh), st5),
            pl.BlockSpec((1, 1, GATE_PAD), st3),
        ],
        out_specs=[
            pl.BlockSpec((1, tb, dmw), fwd_only),
            pl.BlockSpec((1, 2, nh, dh, 2 * dh), st5),
            pl.BlockSpec((1, 1, GATE_PAD), st3),
        ],
        out_shape=[
            jax.ShapeDtypeStruct((bsz, seq, dmw), BF16),
            jax.ShapeDtypeStruct((bsz, 2, nh, dh, 2 * dh), F32),
            jax.ShapeDtypeStruct((bsz, 1, GATE_PAD), F32),
        ],
        scratch_shapes=[
            pltpu.VMEM((seq, dmw), F32),
            pltpu.VMEM((nh, dh, 2 * dh), F32),
            pltpu.VMEM((1, GATE_PAD), F32),
        ],
        compiler_params=_params("arbitrary", "arbitrary", "arbitrary"),
        name="mlstm",
    )(q, kt, v, g, og, zg, c0, m0)


def _conv_geometry(seq):
    nblk = seq // TBLK
    dmax = min(nblk - 1, (seq // 2 + TBLK - 2) // TBLK)
    ncols = (2 * dmax + 1) * TBLK
    n_base = seq // 2 + TBLK * (1 - dmax)
    lpad = n_base + ncols
    return nblk, dmax, ncols, n_base, lpad


def _filter_kernel(w1_ref, b1_ref, fr_ref, w2_ref, b2_ref, w3_ref, rate_ref, o_ref, h2_ref, *, seq, lpad):
    pos = lax.broadcasted_iota(jnp.int32, (1, seq), 1).astype(F32)

    @pl.when(pl.program_id(0) == 0)
    def _():
        bands = (lax.broadcasted_iota(jnp.int32, (FILTER_BANDS, 1), 0) + 1).astype(F32)
        ang = (2.0 * math.pi / seq) * pos * bands
        sn = jnp.sin(ang)
        cs = jnp.cos(ang)
        pre = b1_ref[...] + w1_ref[:, 0:1] * (pos / seq)
        for kb in range(FILTER_BANDS):
            pre = pre + w1_ref[:, 1 + kb:2 + kb] * sn[kb:kb + 1, :]
            pre = pre + w1_ref[:, 1 + FILTER_BANDS + kb:2 + FILTER_BANDS + kb] * cs[kb:kb + 1, :]
        h1 = jnp.sin(fr_ref[:, 0:1] * pre)
        h2_ref[...] = jnp.sin(fr_ref[:, 1:2] * (
            jnp.dot(w2_ref[...], h1, precision=HI, preferred_element_type=F32) + b2_ref[...]))

    f = jnp.dot(w3_ref[...], h2_ref[...], precision=HI, preferred_element_type=F32)
    offset = jnp.abs(pos - (seq // 2)) / (seq / 2)
    f = f * jnp.exp(-offset * rate_ref[...])
    f = f * lax.rsqrt(jnp.sum(f * f, axis=1, keepdims=True) + RMS_EPS)
    rows = f.shape[0]
    first = TBLK // LANES
    for jj in range(lpad // LANES):
        if first <= jj < first + seq // LANES:
            o_ref[jj] = f[:, (jj - first) * LANES:(jj - first + 1) * LANES]
        else:
            o_ref[jj] = jnp.zeros((rows, LANES), F32)


def _filters(seq, w1, b1, freq, w2, b2, w3):
    hidden = w2.shape[0]
    nrow = w3.shape[1]
    c = nrow // HYENA_ORDER
    _, _, _, _, lpad = _conv_geometry(seq)
    rates = jnp.abs(jnp.linspace(MIN_DECAY, MAX_DECAY, c, dtype=F32))
    rates = jnp.tile(rates, HYENA_ORDER).reshape(nrow, 1)
    rb = 2 * LANES
    const = lambda i: (0, 0)
    out = pl.pallas_call(
        functools.partial(_filter_kernel, seq=seq, lpad=lpad),
        grid=(nrow // rb,),
        in_specs=[
            pl.BlockSpec((hidden, w1.shape[0]), const),
            pl.BlockSpec((hidden, 1), const),
            pl.BlockSpec((hidden, 2), const),
            pl.BlockSpec((hidden, hidden), const),
            pl.BlockSpec((hidden, 1), const),
            pl.BlockSpec((rb, hidden), lambda i: (i, 0)),
            pl.BlockSpec((rb, 1), lambda i: (i, 0)),
        ],
        out_specs=pl.BlockSpec((lpad // LANES, rb, LANES), lambda i: (0, i, 0)),
        out_shape=jax.ShapeDtypeStruct((lpad // LANES, nrow, LANES), F32),
        scratch_shapes=[pltpu.VMEM((hidden, seq), F32)],
        compiler_params=_params("arbitrary"),
        name="hyena_filter",
    )(w1.T, b1.reshape(hidden, 1), freq.T, w2.T, b2.reshape(hidden, 1), w3.T, rates)
    return out


def _longconv_kernel(*refs, m, mp, pad, bsz, cb, nch, dmax, nj, jt0, nb, with_zh):
    if with_zh:
        fb_ref, u_ref, x_ref, zh_ref, f_ref, o_ref, p_ref, a_ref, uf_ref, acc_ref = refs
    else:
        fb_ref, u_ref, x_ref, f_ref, o_ref, p_ref, a_ref, uf_ref, acc_ref = refs
        zh_ref = None
    mc = m * cb
    base = pl.program_id(0) * nch
    p_ref[...] = jnp.zeros(p_ref.shape, BF16)
    lane = lax.broadcasted_iota(jnp.int32, (LANES, LANES), 1)
    sub = lax.broadcasted_iota(jnp.int32, (LANES, LANES), 0)
    keep = lane >= sub

    def rows(ref, c):
        return jnp.concatenate([ref[c // cb, pl.ds(hf * mc + c % cb, m, stride=cb), :]
                                for hf in range(TBLK // LANES)], axis=1)

    def stage(c, slot):
        u = rows(u_ref, c)
        uf_ref[slot] = u
        p_ref[slot, 0, pad:pad + m, :] = u.astype(BF16)
        p_ref[slot, 1, pad:pad + bsz + m, :] = jnp.concatenate([jnp.zeros((bsz, TBLK), F32), u], axis=0).astype(BF16)
        prev = None
        for jj in range(jt0 - 2, jt0 + nb):
            tap = jnp.broadcast_to(f_ref[jj, c:c + 1, :], (LANES, LANES))
            cur = pltpu.roll(tap, 0, axis=1, stride=1, stride_axis=0).astype(BF16)
            if prev is not None:
                blk = jnp.where(keep, cur, prev)
                if jj >= jt0:
                    a_ref[slot, :LANES, (jj - jt0) * LANES:(jj - jt0 + 1) * LANES] = blk
                if jj + 1 < jt0 + nb:
                    a_ref[slot, LANES:, (jj + 1 - jt0) * LANES:(jj + 2 - jt0) * LANES] = blk
            prev = cur

    def convolve(slot):
        acc = jnp.zeros((mp, TBLK), F32)
        for d in range(-dmax, dmax + 1):
            start = pad - bsz * (d - d % 2)
            lhs = p_ref[slot, d % 2, start:start + mp, :]
            col = (d + dmax) * TBLK
            acc = acc + jnp.dot(lhs, a_ref[slot, :, col:col + TBLK], preferred_element_type=F32)
        acc_ref[slot] = acc[:m]

    def finish(c, slot):
        y = rows(x_ref, c) * (acc_ref[slot] + uf_ref[slot] * fb_ref[base + c])
        if with_zh:
            y = y * _silu(rows(zh_ref, c))
        for hf in range(TBLK // LANES):
            o_ref[c // cb, pl.ds(hf * mc + c % cb, m, stride=cb), :] = y[:, hf * LANES:(hf + 1) * LANES]

    stage(0, 0)
    stage(1, 1)
    convolve(0)
    for c in range(1, nch):
        finish(c - 1, (c - 1) % CONV_DEPTH)
        if c + 1 < nch:
            stage(c + 1, (c + 1) % CONV_DEPTH)
        convolve(c % CONV_DEPTH)
    finish(nch - 1, (nch - 1) % CONV_DEPTH)


def _longconv(u, u_off, x, x_off, zh, zh_off, filt, f_off, fbias, seq):
    _, nhalf, nblk, bsz, cb, _ = u.shape
    c = fbias.shape[0]
    _, dmax, ncols, n_base, lpad = _conv_geometry(seq)
    nj = lpad // LANES
    m = nblk * bsz
    mp = max(m, 2 * SUBLANES)
    pad = -(-bsz * dmax // BF16_ROWS) * BF16_ROWS
    with_zh = zh is not None
    rows = nhalf * m * cb
    flat = lambda a: a.reshape(a.shape[0], rows, LANES)
    nch = CONV_STEP
    kb = nch // cb
    spec = lambda off: pl.BlockSpec((kb, rows, LANES), lambda i: (off // nch + i, 0, 0))
    in_specs = [pl.BlockSpec(memory_space=pltpu.SMEM), spec(u_off), spec(x_off)]
    args = [fbias, flat(u), flat(x)]
    if with_zh:
        in_specs.append(spec(zh_off))
        args.append(flat(zh))
    in_specs.append(pl.BlockSpec((nj, nch, LANES), lambda i: (0, f_off // nch + i, 0)))
    args.append(filt)
    kern = functools.partial(_longconv_kernel, m=m, mp=mp, pad=pad, bsz=bsz, cb=cb, nch=nch, dmax=dmax, nj=nj,
                             jt0=n_base // LANES, nb=ncols // LANES, with_zh=with_zh)
    out = pl.pallas_call(
        kern,
        grid=(c // nch,),
        in_specs=in_specs,
        out_specs=pl.BlockSpec((kb, rows, LANES), lambda i: (i, 0, 0)),
        out_shape=jax.ShapeDtypeStruct((c // cb, rows, LANES), F32),
        scratch_shapes=[
            pltpu.VMEM((CONV_DEPTH, 2, 2 * pad + bsz + mp, TBLK), BF16),
            pltpu.VMEM((CONV_DEPTH, TBLK, ncols), BF16),
            pltpu.VMEM((CONV_DEPTH, m, TBLK), F32),
            pltpu.VMEM((CONV_DEPTH, m, TBLK), F32),
        ],
        compiler_params=_params("arbitrary"),
        name="hyena_longconv",
    )(*args)
    return out.reshape(c // cb, nhalf, nblk, bsz, cb, LANES)


def _outproj_kernel(m_ref, yh_ref, x_ref, wa_ref, wb_ref, pg_ref, gate_ref, o_ref):
    ncb, nhalf, ntb, _, cb, _ = yh_ref.shape
    for tb in range(ntb):
        rs = slice(tb * TBLK, (tb + 1) * TBLK)
        mm = m_ref[0, rs, :]
        yh = jnp.concatenate([yh_ref[:, hf, tb, 0].reshape(ncb * cb, LANES) for hf in range(nhalf)], axis=1)
        y = (jnp.dot(mm, wa_ref[...], preferred_element_type=F32)
             + jnp.dot(yh.T.astype(BF16), wb_ref[...], preferred_element_type=F32))
        r = y * lax.rsqrt(jnp.mean(y * y, axis=-1, keepdims=True) + RMS_EPS) * pg_ref[...]
        o_ref[0, rs, :] = x_ref[0, rs, :] + gate_ref[0] * r


def _outproj(mm, yh, x, wa, wb, post_gain, gate):
    bsz, seq, d = x.shape
    tm = min(seq, OUTPROJ_ROWS)
    dm = mm.shape[2]
    ncb, nhalf, _, _, cb, _ = yh.shape
    row = lambda b, t: (b, t, 0)
    const2 = lambda b, t: (0, 0)
    return pl.pallas_call(
        _outproj_kernel,
        grid=(bsz, seq // tm),
        in_specs=[
            pl.BlockSpec((1, tm, dm), row),
            pl.BlockSpec((ncb, nhalf, tm // TBLK, 1, cb, LANES), lambda b, t: (0, 0, t, b, 0, 0)),
            pl.BlockSpec((1, tm, d), row),
            pl.BlockSpec(wa.shape, const2, pipeline_mode=pl.Buffered(1)),
            pl.BlockSpec(wb.shape, const2, pipeline_mode=pl.Buffered(1)),
            pl.BlockSpec((1, d), const2),
            pl.BlockSpec((1, 1, d), lambda b, t: (b, 0, 0)),
        ],
        out_specs=pl.BlockSpec((1, tm, d), row),
        out_shape=jax.ShapeDtypeStruct((bsz, seq, d), F32),
        compiler_params=_params("arbitrary", "arbitrary"),
        name="outproj",
    )(mm, yh, x, wa, wb, post_gain, gate)


def _mixer(x, scale, shift, gate, state, lw, *, group, need_out):
    seq = x.shape[1]
    q, kt, v, og, zg, g, hy = _inproj(x, scale, shift, lw["pre_gain"], lw["wn"], lw["wt"], lw["cqw"], lw["cqb"],
                                     lw["chw"], lw["chb"], lw["bg"], lw["mh_gain"], group=group)
    mm, cf, mf = _mlstm(q, kt, v, g, og, zg, *state)
    if not need_out:
        return None, (cf, mf)
    dh = lw["fbias"].shape[1]
    filt = _filters(seq, lw["fw1"], lw["fb1"], lw["ffreq"], lw["fw2"], lw["fb2"], lw["fw3"])
    z = _longconv(hy, 0, hy, dh, None, 0, filt, 0, lw["fbias"][0], seq)
    yh = _longconv(z, 0, hy, 2 * dh, hy, 3 * dh, filt, dh, lw["fbias"][1], seq)
    new_x = _outproj(mm, yh, x, lw["wa"], lw["wb"], lw["post_gain"], gate)
    return new_x, (cf, mf)


def kernel(x, c, ctx, c_ctx, w_mod, b_mod, pre_gain, post_gain, w_in, b_gates, conv_qk_w, conv_qk_b, mh_gain,
           hy_conv_w, hy_conv_b, filt_w1, filt_b1, filt_freq, filt_w2, filt_b2, filt_w3, filt_bias, w_out):
    bsz, _, d = x.shape
    depth = w_mod.shape[0]
    dm = mh_gain.shape[1]
    dh = filt_bias.shape[2]
    nh = MLSTM_HEADS
    hd = dm // nh
    ngate = b_gates.shape[1]
    o_gate = 5 * dm
    o_hy = o_gate + ngate

    mod_rows = 2 * SUBLANES
    cc = jnp.concatenate([c, c_ctx[None, :], jnp.zeros((mod_rows - bsz - 1, d), F32)], axis=0)
    mod = _modulation(cc, w_mod, b_mod)

    zero_state = (jnp.zeros((bsz, 2, nh, hd, 2 * hd), F32), jnp.zeros((bsz, 1, GATE_PAD), F32))
    xl, xc = x, ctx
    for l in range(depth):
        last = l == depth - 1
        w_l = w_in[l].astype(BF16)
        gate_w = jnp.pad(w_l[:, o_gate:o_hy], ((0, 0), (0, GATE_PAD - ngate)))
        lw = dict(
            pre_gain=pre_gain[l].reshape(1, d), post_gain=post_gain[l].reshape(1, d),
            wn=jnp.concatenate([w_l[:, :o_gate], gate_w], axis=1), wt=w_l[:, o_hy:].T,
            cqw=conv_qk_w[l], cqb=conv_qk_b[l].reshape(1, 2 * dm),
            chw=hy_conv_w[l].T, chb=hy_conv_b[l].reshape(3 * dh, 1),
            bg=jnp.pad(b_gates[l], (0, GATE_PAD - ngate)).reshape(1, GATE_PAD),
            mh_gain=mh_gain[l].reshape(1, dm),
            fw1=filt_w1[l], fb1=filt_b1[l], ffreq=filt_freq[l], fw2=filt_w2[l], fb2=filt_b2[l], fw3=filt_w3[l],
            fbias=filt_bias[l], wa=w_out[l, :dm].astype(BF16), wb=w_out[l, dm:].astype(BF16),
        )
        ml = mod[l]
        split3 = lambda a: (a[..., :d], a[..., d:2 * d], a[..., 2 * d:])
        shift_l, scale_l, gate_l = split3(ml[:bsz].reshape(bsz, 1, 3 * d))
        shift_c, scale_c, gate_c = split3(jnp.broadcast_to(ml[bsz].reshape(1, 1, 3 * d), (bsz, 1, 3 * d)))
        new_xc, ctx_state = _mixer(xc, scale_c, shift_c, gate_c, zero_state, lw, group=xc.shape[1], need_out=not last)
        xl, _ = _mixer(xl, scale_l, shift_l, gate_l, ctx_state, lw, group=GRID_W, need_out=True)
        if not last:
            xc = new_xc
    return xl
```
